```python
import functools
import jax, jax.numpy as jnp
from jax import lax
import numpy as np

D_MODEL = 1024
BATCH = 8
SEQ = 2048
DEPTH = 1
DEC_BATCH = 128
DEC_SEQ = 4
PAST_LEN = 16384
PAGE_SIZE = 128

MLA_HEADS = 8
Q_RANK = 384
KV_RANK = 256
NOPE_DIM = 64
ROPE_DIM = 32
V_DIM = 64
ROPE_THETA = 10000.0
SOFTMAX_SCALE = (NOPE_DIM + ROPE_DIM) ** -0.5
Q_BLOCK = 128
RWKV_HEADS = 8
RWKV_HEAD = 64
RWKV_DIM = RWKV_HEADS * RWKV_HEAD
W_LORA = 64
A_LORA = 64
G_LORA = 128
SHIFT_DIM = 3 * RWKV_DIM + W_LORA + A_LORA + G_LORA
GN_EPS = 64e-5
IN_DIM = Q_RANK + KV_RANK + ROPE_DIM + SHIFT_DIM + 2 * D_MODEL
IN_SPLITS = (Q_RANK, Q_RANK + KV_RANK, Q_RANK + KV_RANK + ROPE_DIM, Q_RANK + KV_RANK + ROPE_DIM + SHIFT_DIM)
RWKV_SPLITS = (RWKV_DIM, 2 * RWKV_DIM, 3 * RWKV_DIM, 3 * RWKV_DIM + W_LORA, 3 * RWKV_DIM + W_LORA + A_LORA)
FFN_DIM = 2816
PLE_DIM = 256
NORM_EPS = 1e-6

kernel_name = "mla_rwkv7_gated_macaron_decode_step"


def _rmsnorm(x, g):
    xf = x.astype(jnp.float32)
    y = xf * lax.rsqrt(jnp.mean(xf * xf, axis=-1, keepdims=True) + NORM_EPS)
    return (y * g.astype(jnp.float32)).astype(x.dtype)


def _swiglu(x, wg, wu, wd):
    return (jax.nn.silu(x @ wg) * (x @ wu)) @ wd


def _rope_tables(pos):
    inv = ROPE_THETA ** (-jnp.arange(0, ROPE_DIM, 2, dtype=jnp.float32) / ROPE_DIM)
    ang = pos[:, None] * inv[None, :]
    return jnp.cos(ang), jnp.sin(ang)


def _apply_rope(x, cos, sin):
    xf = x.astype(jnp.float32)
    x1, x2 = jnp.split(xf, 2, axis=-1)
    return jnp.concatenate([x1 * cos - x2 * sin, x1 * sin + x2 * cos], axis=-1).astype(x.dtype)


def _scores(q_lat, q_pe, c_kv, k_pe):
    s = jnp.einsum("bqhr,bkr->bhqk", q_lat, c_kv) + jnp.einsum("bqhe,bke->bhqk", q_pe, k_pe)
    return s.astype(jnp.float32) * SOFTMAX_SCALE


def _attend_causal(q_lat, q_pe, c_kv, k_pe):
    b, s, h, r = q_lat.shape
    nb = s // Q_BLOCK
    qb = q_lat.reshape(b, nb, Q_BLOCK, h, r).swapaxes(0, 1)
    pb = q_pe.reshape(b, nb, Q_BLOCK, h, ROPE_DIM).swapaxes(0, 1)
    k_pos = jnp.arange(s)

    def block(args):
        idx, ql, qp = args
        sc = _scores(ql, qp, c_kv, k_pe)
        q_pos = idx * Q_BLOCK + jnp.arange(Q_BLOCK)
        sc = jnp.where(k_pos[None, :] <= q_pos[:, None], sc, -jnp.inf)
        pr = jax.nn.softmax(sc, axis=-1).astype(c_kv.dtype)
        return jnp.einsum("bhqk,bkr->bqhr", pr, c_kv)

    o = lax.map(block, (jnp.arange(nb), qb, pb))
    return o.swapaxes(0, 1).reshape(b, s, h, r)


def _online(carry, sc, vals):
    m, l, acc = carry
    m_new = jnp.maximum(m, sc.max(axis=-1))
    alpha = jnp.exp(m - m_new)
    pe = jnp.exp(sc - m_new[..., None])
    acc = acc * alpha[..., None] + jnp.einsum("bhqk,bkr->bhqr", pe, vals.astype(jnp.float32))
    return (m_new, l * alpha + pe.sum(axis=-1), acc)


def _attend_paged(q_lat, q_pe, c_kv, k_pe, cache_ckv, cache_kpe, layer, page_table):
    b, t, h, r = q_lat.shape

    def page_step(carry, pages):
        kc = cache_ckv[layer, pages]
        kp = cache_kpe[layer, pages]
        return _online(carry, _scores(q_lat, q_pe, kc, kp), kc), None

    init = (jnp.full((b, h, t), -jnp.inf, jnp.float32),
            jnp.zeros((b, h, t), jnp.float32),
            jnp.zeros((b, h, t, r), jnp.float32))
    carry, _ = lax.scan(page_step, init, page_table.T)
    causal = jnp.tril(jnp.ones((t, t), dtype=bool))
    sc = jnp.where(causal, _scores(q_lat, q_pe, c_kv, k_pe), -jnp.inf)
    _, l, acc = _online(carry, sc, c_kv)
    return (acc / l[..., None]).swapaxes(1, 2).astype(q_lat.dtype)


def _wkv_scan(s0, r, w, k, v, a, bb):
    def step(S, inp):
        rt, wt, kt, vt, at, bt = inp
        sa = jnp.einsum("bhvk,bhk->bhv", S, at)
        S = S * wt[:, :, None, :] + sa[..., None] * bt[:, :, None, :] + vt[..., None] * kt[:, :, None, :]
        return S, jnp.einsum("bhvk,bhk->bhv", S, rt)

    xs = tuple(t_.swapaxes(0, 1) for t_ in (r, w, k, v, a, bb))
    s_fin, ys = lax.scan(step, s0, xs)
    return ys.swapaxes(0, 1), s_fin


def _rwkv7(xr, s0, lw):
    b, s, _ = xr.shape
    r, k, v, xw, xa, xg = jnp.split(xr, RWKV_SPLITS, axis=-1)
    w = -jax.nn.softplus(-(lw["rwkv_w0"] + jnp.tanh(xw) @ lw["rwkv_w_up"])) - 0.5
    decay = jnp.exp(-jnp.exp(w.astype(jnp.float32)))
    a = jax.nn.sigmoid(lw["rwkv_a0"] + xa @ lw["rwkv_a_up"])
    g = jax.nn.sigmoid(xg) @ lw["rwkv_g_up"]

    def hd(t_):
        return t_.reshape(b, s, RWKV_HEADS, RWKV_HEAD).astype(jnp.float32)

    kk = hd(k * lw["rwkv_k_k"])
    kk = kk / jnp.maximum(jnp.sqrt(jnp.sum(kk * kk, axis=-1, keepdims=True)), 1e-12)
    k = k * (1.0 + (a - 1.0) * lw["rwkv_k_a"])
    rh, kh, vh, ah = hd(r), hd(k), hd(v), hd(a)
    y, s_new = _wkv_scan(s0.astype(jnp.float32), rh, hd(decay), kh, vh, -kk, kk * ah)
    mu = jnp.mean(y, axis=-1, keepdims=True)
    var = jnp.mean(jnp.square(y - mu), axis=-1, keepdims=True)
    y = ((y - mu) * lax.rsqrt(var + GN_EPS)).reshape(b, s, RWKV_DIM) * lw["rwkv_ln_w"] + lw["rwkv_ln_b"]
    bonus = jnp.sum(rh * kh * lw["rwkv_r_k"].astype(jnp.float32), axis=-1, keepdims=True) * vh
    y = y + bonus.reshape(b, s, RWKV_DIM)
    out = (y.astype(xr.dtype) * g) @ lw["w_ob"]
    return out, s_new.astype(xr.dtype)


def _layer(x, pemb, cos, sin, shift0, wkv0, attend, lw):
    b, s, _ = x.shape
    h = x
    f = _swiglu(_rmsnorm(h, lw["ffn1_pre_g"]), lw["ffn1_wg"], lw["ffn1_wu"], lw["ffn1_wd"])
    h = h + 0.5 * _rmsnorm(f, lw["ffn1_post_g"])
    u = _rmsnorm(h, lw["mix_pre_g"])
    c_q, c_kv, k_pe, rw, gates = jnp.split(u @ lw["w_in"], IN_SPLITS, axis=-1)
    q = (_rmsnorm(c_q, lw["q_norm_g"]) @ lw["w_uq"]).reshape(b, s, MLA_HEADS, NOPE_DIM + ROPE_DIM)
    q_nope, q_pe = q[..., :NOPE_DIM], q[..., NOPE_DIM:]
    q_pe = _apply_rope(q_pe, cos[None, :, None, :], sin[None, :, None, :])
    c_kv = _rmsnorm(c_kv, lw["kv_norm_g"])
    k_pe = _apply_rope(k_pe, cos[None], sin[None])
    q_lat = jnp.einsum("bshn,rhn->bshr", q_nope, lw["w_uk"])
    o_lat = attend(q_lat, q_pe, c_kv, k_pe)
    o_a = jnp.einsum("bshr,rhv->bshv", o_lat, lw["w_uv"]).reshape(b, s, MLA_HEADS * V_DIM) @ lw["w_oa"]
    prev = jnp.concatenate([shift0[:, None].astype(rw.dtype), rw[:, :-1]], axis=1)
    xr = rw + (prev - rw) * lw["rwkv_mu"]
    o_b, wkv_new = _rwkv7(xr, wkv0, lw)
    new_shift = rw[:, -1]
    g_a, g_b = jnp.split(gates, 2, axis=-1)
    mix = jax.nn.sigmoid(g_a) * o_a + jax.nn.sigmoid(g_b) * o_b
    h = h + _rmsnorm(mix @ lw["w_out"], lw["mix_post_g"])
    f = _swiglu(_rmsnorm(h, lw["ffn2_pre_g"]), lw["ffn2_wg"], lw["ffn2_wu"], lw["ffn2_wd"])
    h = h + 0.5 * _rmsnorm(f, lw["ffn2_post_g"])
    gp = jax.nn.sigmoid(_rmsnorm(h, lw["ple_pre_g"]) @ lw["w_ple_gate"])
    h = h + _rmsnorm(gp * (pemb @ lw["w_ple_proj"]), lw["ple_post_g"])
    return h, c_kv, k_pe, wkv_new, new_shift


def setup_inputs(seed: int = 0) -> dict:
    key = jax.random.key(seed)
    keys = iter(jax.random.split(key, 64))
    f32 = jnp.float32

    def nrm(shape, scale=1.0):
        return jax.random.normal(next(keys), shape, f32) * scale

    def unif(shape, lo, hi):
        return jax.random.uniform(next(keys), shape, f32, lo, hi)

    def gain(n):
        return 1.0 + nrm((DEPTH, n), 0.05)

    n_pages = PAST_LEN // PAGE_SIZE
    n_used = DEC_BATCH * n_pages
    n_pool = n_used + n_used // 4
    page_table = jax.random.permutation(next(keys), n_pool)[:n_used].reshape(DEC_BATCH, n_pages).astype(jnp.int32)
    L = DEPTH
    return {
        "x_prompt": nrm((BATCH, SEQ, D_MODEL)),
        "x_sample": nrm((DEC_BATCH, DEC_SEQ, D_MODEL)),
        "p_prompt": nrm((L, BATCH, SEQ, PLE_DIM)),
        "p_sample": nrm((L, DEC_BATCH, DEC_SEQ, PLE_DIM)),
        "cache_ckv": nrm((L, n_pool, PAGE_SIZE, KV_RANK)),
        "cache_kpe": nrm((L, n_pool, PAGE_SIZE, ROPE_DIM)),
        "state_wkv": nrm((L, DEC_BATCH, RWKV_HEADS, RWKV_HEAD, RWKV_HEAD), 0.1),
        "state_shift": nrm((L, DEC_BATCH, SHIFT_DIM)),
        "page_table": page_table,
        "ffn1_pre_g": gain(D_MODEL),
        "ffn1_wg": nrm((L, D_MODEL, FFN_DIM), D_MODEL ** -0.5),
        "ffn1_wu": nrm((L, D_MODEL, FFN_DIM), D_MODEL ** -0.5),
        "ffn1_wd": nrm((L, FFN_DIM, D_MODEL), FFN_DIM ** -0.5),
        "ffn1_post_g": gain(D_MODEL),
        "mix_pre_g": gain(D_MODEL),
        "w_in": nrm((L, D_MODEL, IN_DIM), D_MODEL ** -0.5),
        "q_norm_g": gain(Q_RANK),
        "w_uq": nrm((L, Q_RANK, MLA_HEADS * (NOPE_DIM + ROPE_DIM)), Q_RANK ** -0.5),
        "kv_norm_g": gain(KV_RANK),
        "w_uk": nrm((L, KV_RANK, MLA_HEADS, NOPE_DIM), KV_RANK ** -0.5),
        "w_uv": nrm((L, KV_RANK, MLA_HEADS, V_DIM), KV_RANK ** -0.5),
        "w_oa": nrm((L, MLA_HEADS * V_DIM, D_MODEL), (MLA_HEADS * V_DIM) ** -0.5),
        "rwkv_mu": unif((L, SHIFT_DIM), 0.0, 1.0),
        "rwkv_w0": unif((L, RWKV_DIM), -6.0, -1.0),
        "rwkv_w_up": nrm((L, W_LORA, RWKV_DIM), 0.1),
        "rwkv_a0": nrm((L, RWKV_DIM), 0.1),
        "rwkv_a_up": nrm((L, A_LORA, RWKV_DIM), A_LORA ** -0.5),
        "rwkv_g_up": nrm((L, G_LORA, RWKV_DIM), G_LORA ** -0.5),
        "rwkv_k_k": 0.85 + nrm((L, RWKV_DIM), 0.05),
        "rwkv_k_a": 1.0 + nrm((L, RWKV_DIM), 0.05),
        "rwkv_r_k": nrm((L, RWKV_HEADS, RWKV_HEAD), 0.1),
        "rwkv_ln_w": gain(RWKV_DIM),
        "rwkv_ln_b": nrm((L, RWKV_DIM), 0.02),
        "w_ob": nrm((L, RWKV_DIM, D_MODEL), RWKV_DIM ** -0.5),
        "w_out": nrm((L, D_MODEL, D_MODEL), D_MODEL ** -0.5),
        "mix_post_g": gain(D_MODEL),
        "ffn2_pre_g": gain(D_MODEL),
        "ffn2_wg": nrm((L, D_MODEL, FFN_DIM), D_MODEL ** -0.5),
        "ffn2_wu": nrm((L, D_MODEL, FFN_DIM), D_MODEL ** -0.5),
        "ffn2_wd": nrm((L, FFN_DIM, D_MODEL), FFN_DIM ** -0.5),
        "ffn2_post_g": gain(D_MODEL),
        "ple_pre_g": gain(D_MODEL),
        "w_ple_gate": nrm((L, D_MODEL, D_MODEL), D_MODEL ** -0.5),
        "w_ple_proj": nrm((L, PLE_DIM, D_MODEL), PLE_DIM ** -0.5),
        "ple_post_g": gain(D_MODEL),
    }


def reference(x_prompt, x_sample, p_prompt, p_sample, cache_ckv, cache_kpe, state_wkv, state_shift, page_table,
              ffn1_pre_g, ffn1_wg, ffn1_wu, ffn1_wd, ffn1_post_g,
              mix_pre_g, w_in, q_norm_g, w_uq, kv_norm_g, w_uk, w_uv, w_oa,
              rwkv_mu, rwkv_w0, rwkv_w_up, rwkv_a0, rwkv_a_up, rwkv_g_up, rwkv_k_k, rwkv_k_a, rwkv_r_k,
              rwkv_ln_w, rwkv_ln_b, w_ob, w_out, mix_post_g,
              ffn2_pre_g, ffn2_wg, ffn2_wu, ffn2_wd, ffn2_post_g,
              ple_pre_g, w_ple_gate, w_ple_proj, ple_post_g):
    past_len = page_table.shape[1] * PAGE_SIZE
    seq = x_prompt.shape[1]
    dec_seq = x_sample.shape[1]
    cos_p, sin_p = _rope_tables(jnp.arange(seq, dtype=jnp.float32))
    cos_s, sin_s = _rope_tables(past_len + jnp.arange(dec_seq, dtype=jnp.float32))
    hp, hs = x_prompt, x_sample
    ckv_p, kpe_p, wkv_p, sh_p = [], [], [], []
    ckv_s, kpe_s, wkv_s, sh_s = [], [], [], []
    for i in range(DEPTH):
        lw = dict(
            ffn1_pre_g=ffn1_pre_g[i], ffn1_wg=ffn1_wg[i], ffn1_wu=ffn1_wu[i], ffn1_wd=ffn1_wd[i],
            ffn1_post_g=ffn1_post_g[i], mix_pre_g=mix_pre_g[i], w_in=w_in[i], q_norm_g=q_norm_g[i],
            w_uq=w_uq[i], kv_norm_g=kv_norm_g[i], w_uk=w_uk[i], w_uv=w_uv[i], w_oa=w_oa[i],
            rwkv_mu=rwkv_mu[i], rwkv_w0=rwkv_w0[i], rwkv_w_up=rwkv_w_up[i], rwkv_a0=rwkv_a0[i],
            rwkv_a_up=rwkv_a_up[i], rwkv_g_up=rwkv_g_up[i], rwkv_k_k=rwkv_k_k[i], rwkv_k_a=rwkv_k_a[i],
            rwkv_r_k=rwkv_r_k[i], rwkv_ln_w=rwkv_ln_w[i], rwkv_ln_b=rwkv_ln_b[i], w_ob=w_ob[i],
            w_out=w_out[i], mix_post_g=mix_post_g[i], ffn2_pre_g=ffn2_pre_g[i], ffn2_wg=ffn2_wg[i],
            ffn2_wu=ffn2_wu[i], ffn2_wd=ffn2_wd[i], ffn2_post_g=ffn2_post_g[i], ple_pre_g=ple_pre_g[i],
            w_ple_gate=w_ple_gate[i], w_ple_proj=w_ple_proj[i], ple_post_g=ple_post_g[i])
        b_p = hp.shape[0]
        zero_shift = jnp.zeros((b_p, SHIFT_DIM), hp.dtype)
        zero_wkv = jnp.zeros((b_p, RWKV_HEADS, RWKV_HEAD, RWKV_HEAD), jnp.float32)
        hp, c1, k1, w1, s1 = _layer(hp, p_prompt[i], cos_p, sin_p, zero_shift, zero_wkv, _attend_causal, lw)
        attend_s = functools.partial(_attend_paged, cache_ckv=cache_ckv, cache_kpe=cache_kpe,
                                     layer=i, page_table=page_table)
        hs, c2, k2, w2, s2 = _layer(hs, p_sample[i], cos_s, sin_s, state_shift[i], state_wkv[i], attend_s, lw)
        ckv_p.append(c1); kpe_p.append(k1); wkv_p.append(w1); sh_p.append(s1)
        ckv_s.append(c2); kpe_s.append(k2); wkv_s.append(w2); sh_s.append(s2)
    return (hp, hs,
            jnp.stack(ckv_p), jnp.stack(kpe_p), jnp.stack(wkv_p), jnp.stack(sh_p),
            jnp.stack(ckv_s), jnp.stack(kpe_s), jnp.stack(wkv_s), jnp.stack(sh_s))
```

```python
import functools
import math

import jax
import jax.numpy as jnp
from jax import lax
from jax.experimental import pallas as pl
from jax.experimental.pallas import tpu as pltpu

F32 = jnp.float32
BF16 = jnp.bfloat16

MLA_HEADS = 8
NOPE_DIM = 64
ROPE_DIM = 32
V_DIM = 64
ROPE_THETA = 10000.0
SOFTMAX_SCALE = (NOPE_DIM + ROPE_DIM) ** -0.5
RWKV_HEADS = 8
RWKV_HEAD = 64
RWKV_DIM = RWKV_HEADS * RWKV_HEAD
W_LORA = 64
A_LORA = 64
G_LORA = 128
GN_EPS = 64e-5
NORM_EPS = 1e-6

LANES = 128
VMEM_LIMIT_BYTES = 56 * 1024 * 1024

ROPE_PAD = LANES
SCAN_BATCH = 8
HALF_HEAD = RWKV_HEAD // 2
MAX_PAGES_PER_CHUNK = 16


def _cparams(*sem):
    return pltpu.CompilerParams(dimension_semantics=sem, vmem_limit_bytes=VMEM_LIMIT_BYTES)


def _dot(a, b):
    return jnp.dot(a, b, preferred_element_type=F32)


def _dot_nt(a, b):
    return lax.dot_general(a, b, (((1,), (1,)), ((), ())), preferred_element_type=F32)


def _rms(x, g):
    return x * lax.rsqrt(jnp.mean(x * x, axis=-1, keepdims=True) + NORM_EPS) * g


def _sigmoid(x):
    return 1.0 / (1.0 + jnp.exp(-x))


def _rope(x, cos, sin_signed):
    lane = lax.broadcasted_iota(jnp.int32, x.shape, 1)
    swapped = jnp.where((lane % ROPE_DIM) < ROPE_DIM // 2,
                        pltpu.roll(x, LANES - ROPE_DIM // 2, 1), pltpu.roll(x, ROPE_DIM // 2, 1))
    return x * cos + swapped * sin_signed


def _segsum(x, ones_bf16):
    hi = x.astype(BF16)
    lo = (x - hi.astype(F32)).astype(BF16)
    return _dot(hi, ones_bf16) + _dot(lo, ones_bf16)


def _row_tile(n, cap):
    t = cap
    while n % t:
        t //= 2
    assert t >= 8, (n, cap)
    return t


def _const_spec(shape):
    nd = len(shape)
    return pl.BlockSpec(shape, lambda *_: (0,) * nd)


def _rows_spec(tm, width):
    return pl.BlockSpec((tm, width), lambda i: (i, 0))


def _ffn_kernel(x_ref, pre_ref, wg_ref, wu_ref, wd_ref, post_ref, o_ref, *, n_chunks):
    x = x_ref[...]
    xn = _rms(x, pre_ref[...]).astype(BF16)
    fc = wg_ref.shape[1] // n_chunks
    acc = None
    for c in range(n_chunks):
        g = _dot(xn, wg_ref[:, c * fc:(c + 1) * fc])
        u = _dot(xn, wu_ref[:, c * fc:(c + 1) * fc])
        a = (g * _sigmoid(g) * u).astype(BF16)
        d = _dot(a, wd_ref[c * fc:(c + 1) * fc, :])
        acc = d if acc is None else acc + d
    o_ref[...] = x + 0.5 * _rms(acc, post_ref[...])


def _ffn(x, pre_g, wg, wu, wd, post_g):
    t, d = x.shape
    f = wg.shape[1]
    tm = _row_tile(t, 512)
    n_chunks = 2 if f % (2 * LANES) == 0 else 1
    return pl.pallas_call(
        functools.partial(_ffn_kernel, n_chunks=n_chunks),
        out_shape=jax.ShapeDtypeStruct((t, d), F32),
        grid=(t // tm,),
        in_specs=[_rows_spec(tm, d), _const_spec((1, d)), _const_spec((d, f)), _const_spec((d, f)),
                  _const_spec((f, d)), _const_spec((1, d))],
        out_specs=_rows_spec(tm, d),
        compiler_params=_cparams("parallel"),
        name="ffn_half_step",
    )(x, pre_g, wg, wu, wd, post_g)


def _inproj_kernel(h_ref, g_ref, wq_ref, wkv_ref, wkpe_ref, wrw_ref, wgate_ref, qng_ref, wuq_ref, kvg_ref,
                   wukt_ref, cos_ref, sin_ref,
                   qlat_ref, qpe_ref, ckv_ref, kpe_ref, rw_ref, gates_ref):
    u = _rms(h_ref[...], g_ref[...]).astype(BF16)
    rw_ref[...] = _dot(u, wrw_ref[...])
    gates_ref[...] = _dot(u, wgate_ref[...])
    ckv_ref[...] = _rms(_dot(u, wkv_ref[...]), kvg_ref[...])
    cos = cos_ref[...]
    sin = sin_ref[...]
    kpe_ref[...] = _rope(_dot(u, wkpe_ref[...]), cos, sin)
    cq = _rms(_dot(u, wq_ref[...]), qng_ref[...]).astype(BF16)
    q = _dot(cq, wuq_ref[...])
    nope_w = MLA_HEADS * NOPE_DIM
    kv_rank = wukt_ref.shape[2]
    for h in range(MLA_HEADS):
        pe = q[:, nope_w + h * ROPE_PAD: nope_w + (h + 1) * ROPE_PAD]
        qpe_ref[:, h * ROPE_PAD:(h + 1) * ROPE_PAD] = _rope(pe, cos, sin).astype(BF16)
        nope = q[:, h * NOPE_DIM:(h + 1) * NOPE_DIM].astype(BF16)
        qlat_ref[:, h * kv_rank:(h + 1) * kv_rank] = _dot(nope, wukt_ref[h]).astype(BF16)


def _inproj(h, g, wq, wkv, wkpe, wrw, wgate, qng, wuq, kvg, wukt, cos, sin):
    t, d = h.shape
    tm = _row_tile(t, 512)
    kv_rank = wkv.shape[1]
    widths = (MLA_HEADS * kv_rank, MLA_HEADS * ROPE_PAD, kv_rank, ROPE_PAD, wrw.shape[1], wgate.shape[1])
    dtypes = (BF16, BF16, F32, F32, F32, F32)
    consts = (g, wq, wkv, wkpe, wrw, wgate, qng, wuq, kvg, wukt)
    return pl.pallas_call(
        _inproj_kernel,
        out_shape=[jax.ShapeDtypeStruct((t, w), dt) for w, dt in zip(widths, dtypes)],
        grid=(t // tm,),
        in_specs=[_rows_spec(tm, d)] + [_const_spec(c.shape) for c in consts]
                 + [_rows_spec(tm, ROPE_PAD), _rows_spec(tm, ROPE_PAD)],
        out_specs=[_rows_spec(tm, w) for w in widths],
        compiler_params=_cparams("parallel"),
        name="mixer_in_proj",
    )(h, *consts, cos, sin)


def _attn_kernel(qlat_ref, qpe_ref, ckv_ref, kpe_ref, wuv_ref, o_ref, q_scr, qp_scr, m_scr, l_scr, acc_scr, *, tq):
    i = pl.program_id(1)
    kv_rank = ckv_ref.shape[1]
    for h in range(MLA_HEADS):
        q_scr[h * tq:(h + 1) * tq, :] = qlat_ref[:, h * kv_rank:(h + 1) * kv_rank]
        qp_scr[h * tq:(h + 1) * tq, :] = qpe_ref[:, h * ROPE_PAD:(h + 1) * ROPE_PAD]
    m_scr[...] = jnp.full(m_scr.shape, -jnp.inf, F32)
    l_scr[...] = jnp.zeros(l_scr.shape, F32)
    acc_scr[...] = jnp.zeros(acc_scr.shape, F32)

    def block(kb, diagonal):
        off = pl.multiple_of(kb * tq, tq)
        kc = ckv_ref[pl.ds(off, tq), :].astype(BF16)
        kp = kpe_ref[pl.ds(off, tq), :].astype(BF16)
        s = (_dot_nt(q_scr[...], kc) + _dot_nt(qp_scr[...], kp)) * SOFTMAX_SCALE
        if diagonal:
            row = lax.broadcasted_iota(jnp.int32, s.shape, 0) % tq
            col = lax.broadcasted_iota(jnp.int32, s.shape, 1)
            s = jnp.where(col <= row, s, -jnp.inf)
        m_prev = m_scr[...]
        m_new = jnp.maximum(m_prev, jnp.max(s, axis=-1, keepdims=True))
        alpha = jnp.exp(m_prev - m_new)
        p = jnp.exp(s - m_new)
        l_scr[...] = alpha * l_scr[...] + jnp.sum(p, axis=-1, keepdims=True)
        acc_scr[...] = alpha * acc_scr[...] + _dot(p.astype(BF16), kc)
        m_scr[...] = m_new

    def off_diagonal(kb, carry):
        block(kb, False)
        return carry

    lax.fori_loop(0, i, off_diagonal, 0)
    block(i, True)
    o = (acc_scr[...] / l_scr[...]).astype(BF16)
    outs = [_dot(o[h * tq:(h + 1) * tq, :], wuv_ref[h]) for h in range(MLA_HEADS)]
    o_ref[...] = jnp.concatenate(outs, axis=-1)


def _attn_prompt(qlat, qpe, ckv, kpe, wuv, batch, seq):
    kv_rank = ckv.shape[1]
    tq = _row_tile(seq, 256)
    nq = seq // tq
    rows = MLA_HEADS * tq
    return pl.pallas_call(
        functools.partial(_attn_kernel, tq=tq),
        out_shape=jax.ShapeDtypeStruct((batch * seq, MLA_HEADS * V_DIM), F32),
        grid=(batch, nq),
        in_specs=[pl.BlockSpec((tq, MLA_HEADS * kv_rank), lambda b, i: (b * nq + i, 0)),
                  pl.BlockSpec((tq, MLA_HEADS * ROPE_PAD), lambda b, i: (b * nq + i, 0)),
                  pl.BlockSpec((seq, kv_rank), lambda b, i: (b, 0)),
                  pl.BlockSpec((seq, ROPE_PAD), lambda b, i: (b, 0)),
                  _const_spec(wuv.shape)],
        out_specs=pl.BlockSpec((tq, MLA_HEADS * V_DIM), lambda b, i: (b * nq + i, 0)),
        scratch_shapes=[pltpu.VMEM((rows, kv_rank), BF16), pltpu.VMEM((rows, ROPE_PAD), BF16),
                        pltpu.VMEM((rows, 1), F32), pltpu.VMEM((rows, 1), F32), pltpu.VMEM((rows, kv_rank), F32)],
        compiler_params=_cparams("parallel", "arbitrary"),
        name="mla_prompt_attention",
    )(qlat, qpe, ckv, kpe, wuv)


def _paged_kernel(pt_ref, q_ref, qpe_ref, knew_ref, kpnew_ref, cache_ckv, cache_kpe, o_ref,
                  kc_buf, kp_buf, sem, *, n_batch, chunks_per_batch, pages_per_chunk, page, dec_seq):
    total = n_batch * chunks_per_batch

    def page_copies(g, slot, p):
        pg = pt_ref[g * pages_per_chunk + p]
        rows = pl.ds(p * page, page)
        return (pltpu.make_async_copy(cache_ckv.at[pg], kc_buf.at[slot, rows, :], sem.at[0, slot]),
                pltpu.make_async_copy(cache_kpe.at[pg], kp_buf.at[slot, rows, :], sem.at[1, slot]))

    def start_chunk(g, slot):
        for p in range(pages_per_chunk):
            for cp in page_copies(g, slot, p):
                cp.start()

    def wait_chunk(g, slot):
        for p in range(pages_per_chunk):
            for cp in page_copies(g, slot, p):
                cp.wait()

    start_chunk(0, 0)

    def batch_body(b, carry):
        qf = q_ref[b]
        qpf = qpe_ref[b][:, :ROPE_DIM]
        qb = qf.astype(BF16)
        qpb = qpf.astype(BF16)
        rows = qf.shape[0]

        def chunk_body(c, st):
            m_prev, l_prev, acc = st
            g = b * chunks_per_batch + c
            slot = g % 2

            @pl.when(g + 1 < total)
            def _():
                start_chunk(g + 1, 1 - slot)

            wait_chunk(g, slot)
            kc = kc_buf[slot].astype(BF16)
            kp = kp_buf[slot].astype(BF16)
            s = (_dot_nt(qb, kc) + _dot_nt(qpb, kp)) * SOFTMAX_SCALE
            m_new = jnp.maximum(m_prev, jnp.max(s, axis=-1, keepdims=True))
            alpha = jnp.exp(m_prev - m_new)
            p = jnp.exp(s - m_new)
            l_new = alpha * l_prev + jnp.sum(p, axis=-1, keepdims=True)
            acc = alpha * acc + _dot(p.astype(BF16), kc)
            return m_new, l_new, acc

        init = (jnp.full((rows, 1), -jnp.inf, F32), jnp.zeros((rows, 1), F32), jnp.zeros(qf.shape, F32))
        m_prev, l_prev, acc = lax.fori_loop(0, chunks_per_batch, chunk_body, init)

        knew = knew_ref[b]
        kpnew = kpnew_ref[b][:, :ROPE_DIM]
        tok = lax.broadcasted_iota(jnp.int32, (rows, 1), 0) % dec_seq
        s_new = []
        for j in range(dec_seq):
            sj = (jnp.sum(qf * knew[j:j + 1, :], axis=-1, keepdims=True)
                  + jnp.sum(qpf * kpnew[j:j + 1, :], axis=-1, keepdims=True)) * SOFTMAX_SCALE
            s_new.append(jnp.where(tok >= j, sj, -jnp.inf))
        m_new = m_prev
        for sj in s_new:
            m_new = jnp.maximum(m_new, sj)
        alpha = jnp.exp(m_prev - m_new)
        l_new = alpha * l_prev
        acc = alpha * acc
        for j, sj in enumerate(s_new):
            pj = jnp.exp(sj - m_new)
            l_new = l_new + pj
            acc = acc + pj * knew[j:j + 1, :]
        o_ref[b] = acc / l_new
        return carry

    lax.fori_loop(0, n_batch, batch_body, 0)


def _attn_paged(page_table, q3, qpe3, knew3, kpnew3, cache_ckv, cache_kpe):
    n_batch, n_pages = page_table.shape
    page = cache_ckv.shape[1]
    kv_rank = cache_ckv.shape[2]
    dec_seq = knew3.shape[1]
    ppc = MAX_PAGES_PER_CHUNK
    while n_pages % ppc:
        ppc //= 2
    kernel = functools.partial(_paged_kernel, n_batch=n_batch, chunks_per_batch=n_pages // ppc,
                               pages_per_chunk=ppc, page=page, dec_seq=dec_seq)
    vmem = lambda shape: pl.BlockSpec(shape, lambda i, pt: (0,) * len(shape))
    any_spec = pl.BlockSpec(memory_space=pl.ANY)
    return pl.pallas_call(
        kernel,
        out_shape=jax.ShapeDtypeStruct(q3.shape, F32),
        grid_spec=pltpu.PrefetchScalarGridSpec(
            num_scalar_prefetch=1,
            grid=(1,),
            in_specs=[vmem(q3.shape), vmem(qpe3.shape), vmem(knew3.shape), vmem(kpnew3.shape), any_spec, any_spec],
            out_specs=vmem(q3.shape),
            scratch_shapes=[pltpu.VMEM((2, ppc * page, kv_rank), F32),
                            pltpu.VMEM((2, ppc * page, cache_kpe.shape[2]), F32),
                            pltpu.SemaphoreType.DMA((2, 2))]),
        compiler_params=_cparams("arbitrary"),
        name="mla_paged_attention",
    )(page_table.reshape(-1), q3, qpe3, knew3, kpnew3, cache_ckv, cache_kpe)


def _uv_kernel(o_ref, wuv_ref, out_ref):
    kv_rank = wuv_ref.shape[1]
    outs = [_dot(o_ref[:, h * kv_rank:(h + 1) * kv_rank].astype(BF16), wuv_ref[h]) for h in range(MLA_HEADS)]
    out_ref[...] = jnp.concatenate(outs, axis=-1)


def _uv_proj(o_lat, wuv):
    t = o_lat.shape[0]
    return pl.pallas_call(
        _uv_kernel,
        out_shape=jax.ShapeDtypeStruct((t, MLA_HEADS * V_DIM), F32),
        grid=(1,),
        in_specs=[_const_spec(o_lat.shape), _const_spec(wuv.shape)],
        out_specs=_const_spec((t, MLA_HEADS * V_DIM)),
        compiler_params=_cparams("arbitrary"),
        name="mla_value_up_proj",
    )(o_lat, wuv)


def _softplus(z):
    return jnp.maximum(z, 0.0) + jnp.log(1.0 + jnp.exp(-jnp.abs(z)))


def _rwkv_prep_kernel(rw_ref, prev_ref, mu_ref, w0_ref, a0_ref, wa_up_ref, g_up_ref, kk_ref, ka_ref, rk_ref, ones_ref,
                      r_o, w_o, k_o, v_o, a_o, b_o, g_o, bonus_o):
    x = rw_ref[...]
    xr = x + (prev_ref[...] - x) * mu_ref[...]
    d = RWKV_DIM
    r = xr[:, 0:d]
    k = xr[:, d:2 * d]
    v = xr[:, 2 * d:3 * d]
    xwa = xr[:, 3 * d:3 * d + W_LORA + A_LORA]
    xg = xr[:, 3 * d + W_LORA + A_LORA:]
    lane = lax.broadcasted_iota(jnp.int32, xwa.shape, 1)
    wa = _dot(jnp.where(lane < W_LORA, jnp.tanh(xwa), xwa).astype(BF16), wa_up_ref[...])
    w = -_softplus(-(w0_ref[...] + wa[:, :d])) - 0.5
    a = _sigmoid(a0_ref[...] + wa[:, d:])
    ones = ones_ref[...]
    kk = k * kk_ref[...]
    kk = kk / jnp.maximum(jnp.sqrt(_segsum(kk * kk, ones)), 1e-12)
    k2 = k * (1.0 + (a - 1.0) * ka_ref[...])
    r_o[...] = r
    w_o[...] = jnp.exp(-jnp.exp(w))
    k_o[...] = k2
    v_o[...] = v
    a_o[...] = -kk
    b_o[...] = kk * a
    g_o[...] = _dot(_sigmoid(xg).astype(BF16), g_up_ref[...])
    bonus_o[...] = _segsum(r * k2 * rk_ref[...], ones) * v


def _rwkv_prep(rw, prev, mu, w0, a0, wa_up, g_up, k_k, k_a, r_k, ones):
    t, width = rw.shape
    tm = _row_tile(t, 512)
    consts = (mu, w0, a0, wa_up, g_up, k_k, k_a, r_k, ones)
    return pl.pallas_call(
        _rwkv_prep_kernel,
        out_shape=[jax.ShapeDtypeStruct((t, RWKV_DIM), F32)] * 8,
        grid=(t // tm,),
        in_specs=[_rows_spec(tm, width), _rows_spec(tm, width)] + [_const_spec(c.shape) for c in consts],
        out_specs=[_rows_spec(tm, RWKV_DIM)] * 8,
        compiler_params=_cparams("parallel"),
        name="rwkv_prep",
    )(rw, prev, *consts)


def _scan_kernel(r_ref, w_ref, k_ref, a_ref, b_ref, v_ref, s0_ref, y_ref, sfin_ref, st_ref, *, ts):
    j = pl.program_id(1)

    @pl.when(j == 0)
    def _():
        st_ref[...] = s0_ref[...]

    n_acc = 4

    def row(ref, t, kq):
        return ref[t, pl.ds(kq, 1), :]

    def step(t, carry):
        parts = [None] * n_acc
        for kq in range(HALF_HEAD):
            term = st_ref[kq] * row(a_ref, t, kq)
            parts[kq % n_acc] = term if parts[kq % n_acc] is None else parts[kq % n_acc] + term
        sa = (parts[0] + parts[1]) + (parts[2] + parts[3])
        sa = sa + pltpu.roll(sa, LANES // 2, 1)
        vt = v_ref[t]
        parts = [None] * n_acc
        for kq in range(HALF_HEAD):
            s_new = st_ref[kq] * row(w_ref, t, kq) + sa * row(b_ref, t, kq) + vt * row(k_ref, t, kq)
            st_ref[kq] = s_new
            term = s_new * row(r_ref, t, kq)
            parts[kq % n_acc] = term if parts[kq % n_acc] is None else parts[kq % n_acc] + term
        y = (parts[0] + parts[1]) + (parts[2] + parts[3])
        y_ref[t] = y + pltpu.roll(y, LANES // 2, 1)
        return carry

    lax.fori_loop(0, ts, step, 0)

    @pl.when(j == pl.num_programs(1) - 1)
    def _():
        sfin_ref[...] = st_ref[...]


def _wkv_scan(r, w, k, a, b, v, s0):
    ng, seq = r.shape[:2]
    ts = _row_tile(seq, 128) if seq % 8 == 0 else seq
    half_spec = pl.BlockSpec((None, ts, HALF_HEAD, LANES), lambda g, j: (g, j, 0, 0))
    full_spec = pl.BlockSpec((None, ts, RWKV_HEAD, LANES), lambda g, j: (g, j, 0, 0))
    state_spec = pl.BlockSpec((None, HALF_HEAD, RWKV_HEAD, LANES), lambda g, j: (g, 0, 0, 0))
    return pl.pallas_call(
        functools.partial(_scan_kernel, ts=ts),
        out_shape=[jax.ShapeDtypeStruct(v.shape, F32), jax.ShapeDtypeStruct(s0.shape, F32)],
        grid=(ng, seq // ts),
        in_specs=[half_spec] * 5 + [full_spec, state_spec],
        out_specs=[full_spec, state_spec],
        scratch_shapes=[pltpu.VMEM((HALF_HEAD, RWKV_HEAD, LANES), F32)],
        compiler_params=_cparams("parallel", "arbitrary"),
        name="rwkv7_state_scan",
    )(r, w, k, a, b, v, s0)


def _to_scan_layout(x, batch, seq, dup):
    ng = batch // SCAN_BATCH
    if dup:
        x = x.reshape(ng, SCAN_BATCH, seq, RWKV_HEADS, RWKV_HEAD).transpose(0, 2, 4, 1, 3)
        x = x.reshape(ng, seq, RWKV_HEAD, SCAN_BATCH * RWKV_HEADS)
        return jnp.concatenate([x, x], axis=-1)
    x = x.reshape(ng, SCAN_BATCH, seq, RWKV_HEADS, 2, HALF_HEAD).transpose(0, 2, 5, 4, 1, 3)
    return x.reshape(ng, seq, HALF_HEAD, LANES)


def _from_scan_layout(y, batch, seq):
    ng = batch // SCAN_BATCH
    y = y[..., :LANES // 2].reshape(ng, seq, RWKV_HEAD, SCAN_BATCH, RWKV_HEADS).transpose(0, 3, 1, 4, 2)
    return y.reshape(batch * seq, RWKV_DIM)


def _state_to_scan_layout(s, batch):
    ng = batch // SCAN_BATCH
    s = s.reshape(ng, SCAN_BATCH, RWKV_HEADS, RWKV_HEAD, 2, HALF_HEAD).transpose(0, 5, 3, 4, 1, 2)
    return s.reshape(ng, HALF_HEAD, RWKV_HEAD, LANES)


def _state_from_scan_layout(s, batch):
    ng = batch // SCAN_BATCH
    s = s.reshape(ng, HALF_HEAD, RWKV_HEAD, 2, SCAN_BATCH, RWKV_HEADS).transpose(0, 4, 5, 2, 3, 1)
    return s.reshape(batch, RWKV_HEADS, RWKV_HEAD, RWKV_HEAD)


def _merge_kernel(h_ref, y_ref, bonus_ref, g_ref, oa_ref, gates_ref, lnw_ref, lnb_ref, ones_ref, wob_ref, woa_ref,
                  wout_ref, post_ref, o_ref):
    ones = ones_ref[...]
    y = y_ref[...]
    inv_n = 1.0 / RWKV_HEAD
    mu = _segsum(y, ones) * inv_n
    yc = y - mu
    var = _segsum(yc * yc, ones) * inv_n
    yn = yc * lax.rsqrt(var + GN_EPS) * lnw_ref[...] + lnb_ref[...] + bonus_ref[...]
    o_b = _dot((yn * g_ref[...]).astype(BF16), wob_ref[...])
    o_a = _dot(oa_ref[...].astype(BF16), woa_ref[...])
    d = o_ref.shape[1]
    gates = gates_ref[...]
    mix = _sigmoid(gates[:, :d]) * o_a + _sigmoid(gates[:, d:]) * o_b
    o_ref[...] = h_ref[...] + _rms(_dot(mix.astype(BF16), wout_ref[...]), post_ref[...])


def _merge(h, y, bonus, g, oa, gates, lnw, lnb, ones, wob, woa, wout, post_g):
    t, d = h.shape
    tm = _row_tile(t, 512)
    consts = (lnw, lnb, ones, wob, woa, wout, post_g)
    return pl.pallas_call(
        _merge_kernel,
        out_shape=jax.ShapeDtypeStruct((t, d), F32),
        grid=(t // tm,),
        in_specs=[_rows_spec(tm, d)] + [_rows_spec(tm, RWKV_DIM)] * 4 + [_rows_spec(tm, 2 * d)]
                 + [_const_spec(c.shape) for c in consts],
        out_specs=_rows_spec(tm, d),
        compiler_params=_cparams("parallel"),
        name="gated_merge_out_proj",
    )(h, y, bonus, g, oa, gates, *consts)


def _ple_kernel(h_ref, p_ref, pre_ref, wgate_ref, wproj_ref, post_ref, o_ref):
    h = h_ref[...]
    gp = _sigmoid(_dot(_rms(h, pre_ref[...]).astype(BF16), wgate_ref[...]))
    emb = _dot(p_ref[...].astype(BF16), wproj_ref[...])
    o_ref[...] = h + _rms(gp * emb, post_ref[...])


def _ple(h, p, pre_g, wgate, wproj, post_g):
    t, d = h.shape
    tm = _row_tile(t, 512)
    consts = (pre_g, wgate, wproj, post_g)
    return pl.pallas_call(
        _ple_kernel,
        out_shape=jax.ShapeDtypeStruct((t, d), F32),
        grid=(t // tm,),
        in_specs=[_rows_spec(tm, d), _rows_spec(tm, p.shape[1])] + [_const_spec(c.shape) for c in consts],
        out_specs=_rows_spec(tm, d),
        compiler_params=_cparams("parallel"),
        name="per_layer_embedding",
    )(h, p, *consts)


def _rope_tables(pos):
    inv = ROPE_THETA ** (-jnp.arange(0, ROPE_DIM, 2, dtype=F32) / ROPE_DIM)
    ang = pos[:, None] * inv[None, :]
    cos, sin = jnp.cos(ang), jnp.sin(ang)
    reps = ROPE_PAD // ROPE_DIM
    return jnp.tile(jnp.concatenate([cos, cos], axis=-1), (1, reps)), jnp.tile(jnp.concatenate([-sin, sin], axis=-1), (1, reps))


def _row(v):
    return v.reshape(1, -1).astype(F32)


def kernel(x_prompt, x_sample, p_prompt, p_sample, cache_ckv, cache_kpe, state_wkv, state_shift, page_table, ffn1_pre_g, ffn1_wg, ffn1_wu, ffn1_wd, ffn1_post_g, mix_pre_g, w_in, q_norm_g, w_uq, kv_norm_g, w_uk, w_uv, w_oa, rwkv_mu, rwkv_w0, rwkv_w_up, rwkv_a0, rwkv_a_up, rwkv_g_up, rwkv_k_k, rwkv_k_a, rwkv_r_k, rwkv_ln_w, rwkv_ln_b, w_ob, w_out, mix_post_g, ffn2_pre_g, ffn2_wg, ffn2_wu, ffn2_wd, ffn2_post_g, ple_pre_g, w_ple_gate, w_ple_proj, ple_post_g):
    depth = w_in.shape[0]
    assert depth == 1, "single-layer step"
    batch, seq, d_model = x_prompt.shape
    dec_batch, dec_seq, _ = x_sample.shape
    n_pages = page_table.shape[1]
    page = cache_ckv.shape[2]
    kv_rank = cache_ckv.shape[3]
    q_rank = w_uq.shape[1]
    shift_dim = state_shift.shape[2]
    tp, ts_tok = batch * seq, dec_batch * dec_seq
    assert batch % SCAN_BATCH == 0 and dec_batch % SCAN_BATCH == 0
    assert shift_dim == 3 * RWKV_DIM + W_LORA + A_LORA + G_LORA
    bf = lambda w: w.astype(BF16)

    w_in0 = w_in[0]
    c0, c1, c2, c3 = q_rank, q_rank + kv_rank, q_rank + kv_rank + ROPE_DIM, q_rank + kv_rank + ROPE_DIM + shift_dim
    wq, wkv, wrw, wgate = bf(w_in0[:, :c0]), bf(w_in0[:, c0:c1]), bf(w_in0[:, c2:c3]), bf(w_in0[:, c3:])
    wkpe = bf(jnp.pad(w_in0[:, c1:c2], ((0, 0), (0, ROPE_PAD - ROPE_DIM))))
    wuq3 = w_uq[0].reshape(q_rank, MLA_HEADS, NOPE_DIM + ROPE_DIM)
    wuq_nope = wuq3[:, :, :NOPE_DIM].reshape(q_rank, MLA_HEADS * NOPE_DIM)
    wuq_pe = jnp.pad(wuq3[:, :, NOPE_DIM:], ((0, 0), (0, 0), (0, ROPE_PAD - ROPE_DIM))).reshape(q_rank, MLA_HEADS * ROPE_PAD)
    wuq = bf(jnp.concatenate([wuq_nope, wuq_pe], axis=1))
    wukt = bf(w_uk[0].transpose(1, 2, 0))
    wuv = bf(w_uv[0].transpose(1, 0, 2))
    wa_up = jnp.zeros((W_LORA + A_LORA, 2 * RWKV_DIM), F32)
    wa_up = bf(wa_up.at[:W_LORA, :RWKV_DIM].set(rwkv_w_up[0]).at[W_LORA:, RWKV_DIM:].set(rwkv_a_up[0]))
    seg = jnp.arange(RWKV_DIM) // RWKV_HEAD
    ones = (seg[:, None] == seg[None, :]).astype(BF16)

    pos = jnp.concatenate([jnp.tile(jnp.arange(seq, dtype=F32), batch),
                           jnp.tile(n_pages * page + jnp.arange(dec_seq, dtype=F32), dec_batch)])
    cos, sin = _rope_tables(pos)

    x = jnp.concatenate([x_prompt.reshape(tp, d_model), x_sample.reshape(ts_tok, d_model)], axis=0)
    pemb = jnp.concatenate([p_prompt[0].reshape(tp, -1), p_sample[0].reshape(ts_tok, -1)], axis=0)

    h1 = _ffn(x, _row(ffn1_pre_g), bf(ffn1_wg[0]), bf(ffn1_wu[0]), bf(ffn1_wd[0]), _row(ffn1_post_g))

    qlat, qpe, ckv, kpe, rw, gates = _inproj(
        h1, _row(mix_pre_g), wq, wkv, wkpe, wrw, wgate, _row(q_norm_g), wuq, _row(kv_norm_g), wukt, cos, sin)

    oa_p = _attn_prompt(qlat, qpe, ckv, kpe, wuv, batch, seq)
    heads_first = lambda a, w: a[tp:].reshape(dec_batch, dec_seq, MLA_HEADS, w).transpose(0, 2, 1, 3).reshape(
        dec_batch, MLA_HEADS * dec_seq, w)
    q3 = heads_first(qlat, kv_rank).astype(F32)
    qpe3 = heads_first(qpe, ROPE_PAD).astype(F32)
    knew3 = ckv[tp:].reshape(dec_batch, dec_seq, kv_rank)
    kpnew3 = kpe[tp:].reshape(dec_batch, dec_seq, ROPE_PAD)
    o3 = _attn_paged(page_table, q3, qpe3, knew3, kpnew3, cache_ckv[0], cache_kpe[0])
    o_lat_s = o3.reshape(dec_batch, MLA_HEADS, dec_seq, kv_rank).transpose(0, 2, 1, 3).reshape(ts_tok, MLA_HEADS * kv_rank)
    oa = jnp.concatenate([oa_p, _uv_proj(o_lat_s, wuv)], axis=0)

    rw_p = rw[:tp].reshape(batch, seq, shift_dim)
    rw_s = rw[tp:].reshape(dec_batch, dec_seq, shift_dim)
    prev = jnp.concatenate([
        jnp.concatenate([jnp.zeros((batch, 1, shift_dim), F32), rw_p[:, :-1]], axis=1).reshape(tp, shift_dim),
        jnp.concatenate([state_shift[0][:, None], rw_s[:, :-1]], axis=1).reshape(ts_tok, shift_dim)], axis=0)
    r, w, k, v, a, b, g, bonus = _rwkv_prep(
        rw, prev, _row(rwkv_mu), _row(rwkv_w0), _row(rwkv_a0), wa_up, bf(rwkv_g_up[0]), _row(rwkv_k_k),
        _row(rwkv_k_a), _row(rwkv_r_k), ones)

    def scan_group(lo, hi, nb, ns, s0):
        halves = [_to_scan_layout(t_[lo:hi], nb, ns, False) for t_ in (r, w, k, a, b)]
        y, s_fin = _wkv_scan(*halves, _to_scan_layout(v[lo:hi], nb, ns, True), _state_to_scan_layout(s0, nb))
        return _from_scan_layout(y, nb, ns), _state_from_scan_layout(s_fin, nb)

    y_p, wkv_p = scan_group(0, tp, batch, seq, jnp.zeros((batch, RWKV_HEADS, RWKV_HEAD, RWKV_HEAD), F32))
    y_s, wkv_s = scan_group(tp, tp + ts_tok, dec_batch, dec_seq, state_wkv[0].astype(F32))
    y = jnp.concatenate([y_p, y_s], axis=0)

    h2 = _merge(h1, y, bonus, g, oa, gates, _row(rwkv_ln_w), _row(rwkv_ln_b), ones, bf(w_ob[0]), bf(w_oa[0]),
                bf(w_out[0]), _row(mix_post_g))
    h3 = _ffn(h2, _row(ffn2_pre_g), bf(ffn2_wg[0]), bf(ffn2_wu[0]), bf(ffn2_wd[0]), _row(ffn2_post_g))
    h4 = _ple(h3, pemb, _row(ple_pre_g), bf(w_ple_gate[0]), bf(w_ple_proj[0]), _row(ple_post_g))

    return (h4[:tp].reshape(batch, seq, d_model), h4[tp:].reshape(dec_batch, dec_seq, d_model),
            ckv[:tp].reshape(1, batch, seq, kv_rank), kpe[:tp, :ROPE_DIM].reshape(1, batch, seq, ROPE_DIM),
            wkv_p[None], rw_p[:, -1][None],
            ckv[tp:].reshape(1, dec_batch, dec_seq, kv_rank), kpe[tp:, :ROPE_DIM].reshape(1, dec_batch, dec_seq, ROPE_DIM),
            wkv_s[None], rw_s[:, -1][None])
```

```python
import functools
import math

import jax
import jax.numpy as jnp
from jax import lax
from jax.experimental import pallas as pl
from jax.experimental.pallas import tpu as pltpu

F32 = jnp.float32
BF16 = jnp.bfloat16

MLA_HEADS = 8
NOPE_DIM = 64
ROPE_DIM = 32
V_DIM = 64
ROPE_THETA = 10000.0
SOFTMAX_SCALE = (NOPE_DIM + ROPE_DIM) ** -0.5
Q_SCALE = SOFTMAX_SCALE * math.log2(math.e)
RWKV_HEADS = 8
RWKV_HEAD = 64
RWKV_DIM = RWKV_HEADS * RWKV_HEAD
W_LORA = 64
A_LORA = 64
G_LORA = 128
GN_EPS = 64e-5
NORM_EPS = 1e-6

LANES = 128
VMEM_LIMIT_BYTES = 56 * 1024 * 1024

ROPE_PAD = LANES
SCAN_BATCH = 8
HALF_HEAD = RWKV_HEAD // 2
MAX_PAGES_PER_STREAM = 16
PAGED_STREAMS = 2


def _cparams(*sem):
    return pltpu.CompilerParams(dimension_semantics=sem, vmem_limit_bytes=VMEM_LIMIT_BYTES)


def _dot(a, b):
    return jnp.dot(a, b, preferred_element_type=F32)


def _dot_nt(a, b):
    return lax.dot_general(a, b, (((1,), (1,)), ((), ())), preferred_element_type=F32)


def _rms(x, g):
    return x * lax.rsqrt(jnp.mean(x * x, axis=-1, keepdims=True) + NORM_EPS) * g


def _sigmoid(x):
    return 1.0 / (1.0 + jnp.exp(-x))


def _rope(x, cos, sin_signed):
    lane = lax.broadcasted_iota(jnp.int32, x.shape, 1)
    swapped = jnp.where((lane % ROPE_DIM) < ROPE_DIM // 2,
                        pltpu.roll(x, LANES - ROPE_DIM // 2, 1), pltpu.roll(x, ROPE_DIM // 2, 1))
    return x * cos + swapped * sin_signed


def _segsum(x, ones_bf16):
    hi = x.astype(BF16)
    lo = (x - hi.astype(F32)).astype(BF16)
    return _dot(hi, ones_bf16) + _dot(lo, ones_bf16)


def _row_tile(n, cap):
    t = cap
    while n % t:
        t //= 2
    assert t >= 8, (n, cap)
    return t


def _const_spec(shape):
    nd = len(shape)
    return pl.BlockSpec(shape, lambda *_: (0,) * nd)


def _rows_spec(tm, width):
    return pl.BlockSpec((tm, width), lambda i: (i, 0))


def _ffn_kernel(x_ref, pre_ref, wg_ref, wu_ref, wd_ref, post_ref, o_ref, *, n_chunks):
    x = x_ref[...]
    xn = _rms(x, pre_ref[...]).astype(BF16)
    fc = wg_ref.shape[1] // n_chunks
    acc = None
    for c in range(n_chunks):
        g = _dot(xn, wg_ref[:, c * fc:(c + 1) * fc])
        u = _dot(xn, wu_ref[:, c * fc:(c + 1) * fc])
        a = (g * _sigmoid(g) * u).astype(BF16)
        d = _dot(a, wd_ref[c * fc:(c + 1) * fc, :])
        acc = d if acc is None else acc + d
    o_ref[...] = x + 0.5 * _rms(acc, post_ref[...])


def _ffn(x, pre_g, wg, wu, wd, post_g):
    t, d = x.shape
    f = wg.shape[1]
    tm = _row_tile(t, 512)
    n_chunks = 2 if f % (2 * LANES) == 0 else 1
    return pl.pallas_call(
        functools.partial(_ffn_kernel, n_chunks=n_chunks),
        out_shape=jax.ShapeDtypeStruct((t, d), F32),
        grid=(t // tm,),
        in_specs=[_rows_spec(tm, d), _const_spec((1, d)), _const_spec((d, f)), _const_spec((d, f)),
                  _const_spec((f, d)), _const_spec((1, d))],
        out_specs=_rows_spec(tm, d),
        compiler_params=_cparams("parallel"),
        name="ffn_half_step",
    )(x, pre_g, wg, wu, wd, post_g)


def _inproj_kernel(h_ref, g_ref, wq_ref, wkv_ref, wkpe_ref, wrw_ref, wgate_ref, qng_ref, wuq_ref, kvg_ref,
                   wukt_ref, cos_ref, sin_ref, qcat_ref, ckv_ref, kpe_ref, rw_ref, gates_ref, *kt_refs, tk):
    u = _rms(h_ref[...], g_ref[...]).astype(BF16)
    rw_ref[...] = _dot(u, wrw_ref[...])
    gates_ref[...] = _dot(u, wgate_ref[...])
    ckv = _rms(_dot(u, wkv_ref[...]), kvg_ref[...])
    ckv_ref[...] = ckv
    cos = cos_ref[...]
    sin = sin_ref[...]
    kpe = _rope(_dot(u, wkpe_ref[...]), cos, sin)
    kpe_ref[...] = kpe[:, :ROPE_DIM]
    kv_rank = ckv.shape[1]
    if kt_refs:
        kcat_ref, ckvt_ref = kt_refs
        kcat_ref[:, :kv_rank] = ckv.astype(BF16)
        kcat_ref[:, kv_rank:] = kpe.astype(BF16)
        for j in range(ckv.shape[0] // tk):
            ckvt_ref[j] = ckv[j * tk:(j + 1) * tk, :].T.astype(BF16)
    cq = _rms(_dot(u, wq_ref[...]), qng_ref[...]).astype(BF16)
    q = _dot(cq, wuq_ref[...])
    nope_w = MLA_HEADS * NOPE_DIM
    qw = kv_rank + ROPE_PAD
    for h in range(MLA_HEADS):
        nope = q[:, h * NOPE_DIM:(h + 1) * NOPE_DIM].astype(BF16)
        qcat_ref[:, h * qw:h * qw + kv_rank] = (_dot(nope, wukt_ref[h]) * Q_SCALE).astype(BF16)
        pe = q[:, nope_w + h * ROPE_PAD: nope_w + (h + 1) * ROPE_PAD]
        qcat_ref[:, h * qw + kv_rank:(h + 1) * qw] = (_rope(pe, cos, sin) * Q_SCALE).astype(BF16)


def _inproj(h, g, wq, wkv, wkpe, wrw, wgate, qng, wuq, kvg, wukt, cos, sin, tk):
    t, d = h.shape
    tm = _row_tile(t, 512)
    kv_rank = wkv.shape[1]
    qw = kv_rank + ROPE_PAD
    n_pos = cos.shape[0] // tm
    out_shape = [jax.ShapeDtypeStruct((t, MLA_HEADS * qw), BF16), jax.ShapeDtypeStruct((t, kv_rank), F32),
                 jax.ShapeDtypeStruct((t, ROPE_DIM), F32), jax.ShapeDtypeStruct((t, wrw.shape[1]), F32),
                 jax.ShapeDtypeStruct((t, wgate.shape[1]), F32)]
    out_specs = [_rows_spec(tm, s.shape[1]) for s in out_shape]
    if tk:
        assert tm % tk == 0
        out_shape += [jax.ShapeDtypeStruct((t, qw), BF16), jax.ShapeDtypeStruct((t // tk, kv_rank, tk), BF16)]
        out_specs += [_rows_spec(tm, qw), pl.BlockSpec((tm // tk, kv_rank, tk), lambda i: (i, 0, 0))]
    consts = (g, wq, wkv, wkpe, wrw, wgate, qng, wuq, kvg, wukt)
    pos_spec = pl.BlockSpec((tm, ROPE_PAD), lambda i: (i % n_pos, 0))
    return pl.pallas_call(
        functools.partial(_inproj_kernel, tk=tk),
        out_shape=out_shape,
        grid=(t // tm,),
        in_specs=[_rows_spec(tm, d)] + [_const_spec(c.shape) for c in consts] + [pos_spec, pos_spec],
        out_specs=out_specs,
        compiler_params=_cparams("parallel"),
        name="mixer_in_proj",
    )(h, *consts, cos, sin)


def _attn_kernel(qcat_ref, kcat_ref, ckvt_ref, wuvt_ref, o_ref, m_scr, l_scr, acc_scr, *, tq):
    i = pl.program_id(1)
    qw = kcat_ref.shape[1]
    m_scr[...] = jnp.full(m_scr.shape, -jnp.inf, F32)
    l_scr[...] = jnp.zeros(l_scr.shape, F32)
    acc_scr[...] = jnp.zeros(acc_scr.shape, F32)

    def block(kb, diagonal):
        off = pl.multiple_of(kb * tq, tq)
        kcat = kcat_ref[pl.ds(off, tq), :]
        vt = ckvt_ref[kb]
        for h in range(MLA_HEADS):
            s = _dot_nt(kcat, qcat_ref[:, h * qw:(h + 1) * qw])
            if diagonal:
                key = lax.broadcasted_iota(jnp.int32, s.shape, 0)
                qry = lax.broadcasted_iota(jnp.int32, s.shape, 1)
                s = jnp.where(key <= qry, s, -jnp.inf)
            m_prev = m_scr[h]
            m_new = jnp.maximum(m_prev, jnp.max(s, axis=0, keepdims=True))
            alpha = jnp.exp2(m_prev - m_new)
            p = jnp.exp2(s - m_new)
            l_scr[h] = alpha * l_scr[h] + jnp.sum(p, axis=0, keepdims=True)
            acc_scr[h] = alpha * acc_scr[h] + _dot(vt, p.astype(BF16))
            m_scr[h] = m_new

    def off_diagonal(kb, carry):
        block(kb, False)
        return carry

    lax.fori_loop(0, i, off_diagonal, 0)
    block(i, True)
    outs = [_dot(wuvt_ref[h], (acc_scr[h] / l_scr[h]).astype(BF16)) for h in range(MLA_HEADS)]
    o_ref[...] = jnp.concatenate(outs, axis=0).T


def _attn_prompt(qcat, kcat, ckvt, wuvt, batch, seq, tq):
    qw = kcat.shape[1]
    kv_rank = ckvt.shape[1]
    nq = seq // tq
    return pl.pallas_call(
        functools.partial(_attn_kernel, tq=tq),
        out_shape=jax.ShapeDtypeStruct((batch * seq, MLA_HEADS * V_DIM), F32),
        grid=(batch, nq),
        in_specs=[pl.BlockSpec((tq, MLA_HEADS * qw), lambda b, i: (b * nq + i, 0)),
                  pl.BlockSpec((seq, qw), lambda b, i: (b, 0)),
                  pl.BlockSpec((nq, kv_rank, tq), lambda b, i: (b, 0, 0)),
                  _const_spec(wuvt.shape)],
        out_specs=pl.BlockSpec((tq, MLA_HEADS * V_DIM), lambda b, i: (b * nq + i, 0)),
        scratch_shapes=[pltpu.VMEM((MLA_HEADS, 1, tq), F32), pltpu.VMEM((MLA_HEADS, 1, tq), F32),
                        pltpu.VMEM((MLA_HEADS, kv_rank, tq), F32)],
        compiler_params=_cparams("parallel", "arbitrary"),
        name="mla_prompt_attention",
    )(qcat, kcat, ckvt, wuvt)


def _paged_kernel(pt_ref, q_ref, knew_ref, kpnew_ref, cache_ckv, cache_kpe, o_ref,
                  kc_buf, kp_buf, sem, *, n_batch, chunks_per_batch, pages_per_stream, page, dec_seq):
    total = n_batch * chunks_per_batch
    pages_per_chunk = PAGED_STREAMS * pages_per_stream
    kv_rank = kc_buf.shape[2]

    def page_copies(g, slot, p):
        pg = pt_ref[g * pages_per_chunk + p]
        rows = pl.ds(p * page, page)
        return (pltpu.make_async_copy(cache_ckv.at[0, pg], kc_buf.at[slot, rows, :], sem.at[0, slot]),
                pltpu.make_async_copy(cache_kpe.at[0, pg], kp_buf.at[slot, rows, :], sem.at[1, slot]))

    def start_chunk(g, slot):
        for p in range(pages_per_chunk):
            for cp in page_copies(g, slot, p):
                cp.start()

    def wait_chunk(g, slot):
        for p in range(pages_per_chunk):
            for cp in page_copies(g, slot, p):
                cp.wait()

    start_chunk(0, 0)

    def batch_body(b, carry):
        qf = q_ref[b]
        qlf = qf[:, :kv_rank]
        qpf = qf[:, kv_rank:kv_rank + ROPE_DIM]
        qb = qlf.astype(BF16)
        qpb = qpf.astype(BF16)
        rows = qf.shape[0]
        stream_rows = pages_per_stream * page

        def chunk_body(c, states):
            g = b * chunks_per_batch + c
            slot = g % 2

            @pl.when(g + 1 < total)
            def _():
                start_chunk(g + 1, 1 - slot)

            wait_chunk(g, slot)
            new_states = []
            for st, (m_prev, l_prev, acc) in enumerate(states):
                kc = kc_buf[slot, st * stream_rows:(st + 1) * stream_rows, :].astype(BF16)
                kp = kp_buf[slot, st * stream_rows:(st + 1) * stream_rows, :].astype(BF16)
                s = _dot_nt(qb, kc) + _dot_nt(qpb, kp)
                m_new = jnp.maximum(m_prev, jnp.max(s, axis=-1, keepdims=True))
                alpha = jnp.exp2(m_prev - m_new)
                p = jnp.exp2(s - m_new)
                l_new = alpha * l_prev + jnp.sum(p, axis=-1, keepdims=True)
                new_states.append((m_new, l_new, alpha * acc + _dot(p.astype(BF16), kc)))
            return tuple(new_states)

        init = tuple((jnp.full((rows, 1), -jnp.inf, F32), jnp.zeros((rows, 1), F32), jnp.zeros(qlf.shape, F32))
                     for _ in range(PAGED_STREAMS))
        states = lax.fori_loop(0, chunks_per_batch, chunk_body, init)

        knew = knew_ref[b]
        kpnew = kpnew_ref[b]
        tok = lax.broadcasted_iota(jnp.int32, (rows, 1), 0) % dec_seq
        s_new = []
        for j in range(dec_seq):
            sj = (jnp.sum(qlf * knew[j:j + 1, :], axis=-1, keepdims=True)
                  + jnp.sum(qpf * kpnew[j:j + 1, :], axis=-1, keepdims=True))
            s_new.append(jnp.where(tok >= j, sj, -jnp.inf))
        m_new = states[0][0]
        for m_st, _, _ in states[1:]:
            m_new = jnp.maximum(m_new, m_st)
        for sj in s_new:
            m_new = jnp.maximum(m_new, sj)
        l_new = jnp.zeros((rows, 1), F32)
        acc = jnp.zeros(qlf.shape, F32)
        for m_st, l_st, acc_st in states:
            alpha = jnp.exp2(m_st - m_new)
            l_new = l_new + alpha * l_st
            acc = acc + alpha * acc_st
        for j, sj in enumerate(s_new):
            pj = jnp.exp2(sj - m_new)
            l_new = l_new + pj
            acc = acc + pj * knew[j:j + 1, :]
        o_ref[b] = acc / l_new
        return carry

    lax.fori_loop(0, n_batch, batch_body, 0)


def _attn_paged(page_table, q3, knew3, kpnew3, cache_ckv, cache_kpe):
    n_batch, n_pages = page_table.shape
    page = cache_ckv.shape[2]
    kv_rank = cache_ckv.shape[3]
    dec_seq = knew3.shape[1]
    pps = MAX_PAGES_PER_STREAM
    while n_pages % (PAGED_STREAMS * pps):
        pps //= 2
    assert pps >= 1, n_pages
    ppc = PAGED_STREAMS * pps
    kernel = functools.partial(_paged_kernel, n_batch=n_batch, chunks_per_batch=n_pages // ppc,
                               pages_per_stream=pps, page=page, dec_seq=dec_seq)
    vmem = lambda shape: pl.BlockSpec(shape, lambda i, pt: (0,) * len(shape))
    any_spec = pl.BlockSpec(memory_space=pl.ANY)
    out_shape = (n_batch, q3.shape[1], kv_rank)
    return pl.pallas_call(
        kernel,
        out_shape=jax.ShapeDtypeStruct(out_shape, F32),
        grid_spec=pltpu.PrefetchScalarGridSpec(
            num_scalar_prefetch=1,
            grid=(1,),
            in_specs=[vmem(q3.shape), vmem(knew3.shape), vmem(kpnew3.shape), any_spec, any_spec],
            out_specs=vmem(out_shape),
            scratch_shapes=[pltpu.VMEM((2, ppc * page, kv_rank), F32),
                            pltpu.VMEM((2, ppc * page, cache_kpe.shape[3]), F32),
                            pltpu.SemaphoreType.DMA((2, 2))]),
        compiler_params=_cparams("arbitrary"),
        name="mla_paged_attention",
    )(page_table.reshape(-1), q3, knew3, kpnew3, cache_ckv, cache_kpe)


def _uv_kernel(o_ref, wuv_ref, out_ref):
    kv_rank = wuv_ref.shape[1]
    outs = [_dot(o_ref[:, h * kv_rank:(h + 1) * kv_rank].astype(BF16), wuv_ref[h]) for h in range(MLA_HEADS)]
    out_ref[...] = jnp.concatenate(outs, axis=-1)


def _uv_proj(o_lat, wuv):
    t = o_lat.shape[0]
    return pl.pallas_call(
        _uv_kernel,
        out_shape=jax.ShapeDtypeStruct((t, MLA_HEADS * V_DIM), F32),
        grid=(1,),
        in_specs=[_const_spec(o_lat.shape), _const_spec(wuv.shape)],
        out_specs=_const_spec((t, MLA_HEADS * V_DIM)),
        compiler_params=_cparams("arbitrary"),
        name="mla_value_up_proj",
    )(o_lat, wuv)


def _softplus(z):
    return jnp.maximum(z, 0.0) + jnp.log(1.0 + jnp.exp(-jnp.abs(z)))


def _rwkv_prep_kernel(rw_ref, first_ref, *refs, seq, tiles_per_seq):
    x = rw_ref[...]
    row = lax.broadcasted_iota(jnp.int32, x.shape, 0)
    rolled = pltpu.roll(x, 1, 0)
    if tiles_per_seq:
        prev8_ref, refs = refs[0], refs[1:]
        at_start = pl.program_id(0) % tiles_per_seq == 0
        first = jnp.where(at_start, first_ref[...], prev8_ref[7:8, :])
        prev = jnp.where(row == 0, first, rolled)
    else:
        prev = jnp.where(row % seq == 0, first_ref[...], rolled)
    (mu_ref, w0_ref, a0_ref, wa_up_ref, g_up_ref, kk_ref, ka_ref, rk_ref, ones_ref,
     r_o, w_o, k_o, v_o, a_o, b_o, g_o, bonus_o) = refs
    xr = x + (prev - x) * mu_ref[...]
    d = RWKV_DIM
    r = xr[:, 0:d]
    k = xr[:, d:2 * d]
    v = xr[:, 2 * d:3 * d]
    xwa = xr[:, 3 * d:3 * d + W_LORA + A_LORA]
    xg = xr[:, 3 * d + W_LORA + A_LORA:]
    lane = lax.broadcasted_iota(jnp.int32, xwa.shape, 1)
    wa = _dot(jnp.where(lane < W_LORA, jnp.tanh(xwa), xwa).astype(BF16), wa_up_ref[...])
    w = -_softplus(-(w0_ref[...] + wa[:, :d])) - 0.5
    a = _sigmoid(a0_ref[...] + wa[:, d:])
    ones = ones_ref[...]
    kk = k * kk_ref[...]
    kk = kk / jnp.maximum(jnp.sqrt(_segsum(kk * kk, ones)), 1e-12)
    k2 = k * (1.0 + (a - 1.0) * ka_ref[...])
    r_o[...] = r
    w_o[...] = jnp.exp(-jnp.exp(w))
    k_o[...] = k2
    v_o[...] = v
    a_o[...] = -kk
    b_o[...] = kk * a
    g_o[...] = _dot(_sigmoid(xg).astype(BF16), g_up_ref[...])
    bonus_o[...] = _segsum(r * k2 * rk_ref[...], ones) * v


def _rwkv_prep(rw, shift0, seq, mu, w0, a0, wa_up, g_up, k_k, k_a, r_k, ones):
    t, width = rw.shape
    tm = _row_tile(t, 512)
    consts = (mu, w0, a0, wa_up, g_up, k_k, k_a, r_k, ones)
    if seq % tm == 0:
        tiles_per_seq = seq // tm
        first = shift0.reshape(-1, 1, width)
        lead = [first, rw]
        lead_specs = [pl.BlockSpec((None, 1, width), lambda i: (i // tiles_per_seq, 0, 0)),
                      pl.BlockSpec((8, width), lambda i: (jnp.maximum(i * (tm // 8) - 1, 0), 0))]
    else:
        assert tm % seq == 0
        tiles_per_seq = 0
        lead = [jnp.repeat(shift0, seq, axis=0)]
        lead_specs = [_rows_spec(tm, width)]
    return pl.pallas_call(
        functools.partial(_rwkv_prep_kernel, seq=seq, tiles_per_seq=tiles_per_seq),
        out_shape=[jax.ShapeDtypeStruct((t, RWKV_DIM), F32)] * 8,
        grid=(t // tm,),
        in_specs=[_rows_spec(tm, width)] + lead_specs + [_const_spec(c.shape) for c in consts],
        out_specs=[_rows_spec(tm, RWKV_DIM)] * 8,
        compiler_params=_cparams("parallel"),
        name="rwkv_prep",
    )(rw, *lead, *consts)


def _scan_kernel(r_ref, w_ref, k_ref, a_ref, b_ref, v_ref, s0_ref, y_ref, sfin_ref, st_ref, *, ts):
    j = pl.program_id(1)

    @pl.when(j == 0)
    def _():
        st_ref[...] = s0_ref[...]

    n_acc = 2

    def row(ref, t, kq):
        return ref[t, pl.ds(kq, 1), :]

    def step(t, carry):
        parts = [None] * n_acc
        for kq in range(HALF_HEAD):
            term = st_ref[kq] * row(a_ref, t, kq)
            parts[kq % n_acc] = term if parts[kq % n_acc] is None else parts[kq % n_acc] + term
        sa = parts[0] + parts[1]
        sa = sa + pltpu.roll(sa, LANES // 2, 1)
        vt = v_ref[t]
        parts = [None] * n_acc
        for kq in range(HALF_HEAD):
            s_new = st_ref[kq] * row(w_ref, t, kq) + sa * row(b_ref, t, kq) + vt * row(k_ref, t, kq)
            st_ref[kq] = s_new
            term = s_new * row(r_ref, t, kq)
            parts[kq % n_acc] = term if parts[kq % n_acc] is None else parts[kq % n_acc] + term
        y = parts[0] + parts[1]
        y_ref[t] = y + pltpu.roll(y, LANES // 2, 1)
        return carry

    lax.fori_loop(0, ts, step, 0)

    @pl.when(j == pl.num_programs(1) - 1)
    def _():
        sfin_ref[...] = st_ref[...]


def _wkv_scan(r, w, k, a, b, v, s0):
    ng, seq = r.shape[:2]
    ts = _row_tile(seq, 128) if seq % 8 == 0 else seq
    half_spec = pl.BlockSpec((None, ts, HALF_HEAD, LANES), lambda g, j: (g, j, 0, 0))
    full_spec = pl.BlockSpec((None, ts, RWKV_HEAD, LANES), lambda g, j: (g, j, 0, 0))
    state_spec = pl.BlockSpec((None, HALF_HEAD, RWKV_HEAD, LANES), lambda g, j: (g, 0, 0, 0))
    return pl.pallas_call(
        functools.partial(_scan_kernel, ts=ts),
        out_shape=[jax.ShapeDtypeStruct(v.shape, F32), jax.ShapeDtypeStruct(s0.shape, F32)],
        grid=(ng, seq // ts),
        in_specs=[half_spec] * 5 + [full_spec, state_spec],
        out_specs=[full_spec, state_spec],
        scratch_shapes=[pltpu.VMEM((HALF_HEAD, RWKV_HEAD, LANES), F32)],
        compiler_params=_cparams("parallel", "arbitrary"),
        name="rwkv7_state_scan",
    )(r, w, k, a, b, v, s0)


def _to_scan_layout(x, batch, seq, dup):
    ng = batch // SCAN_BATCH
    if dup:
        x = x.reshape(ng, SCAN_BATCH, seq, RWKV_HEADS, RWKV_HEAD).transpose(0, 2, 4, 1, 3)
        x = x.reshape(ng, seq, RWKV_HEAD, SCAN_BATCH * RWKV_HEADS)
        return jnp.concatenate([x, x], axis=-1)
    x = x.reshape(ng, SCAN_BATCH, seq, RWKV_HEADS, 2, HALF_HEAD).transpose(0, 2, 5, 4, 1, 3)
    return x.reshape(ng, seq, HALF_HEAD, LANES)


def _from_scan_layout(y, batch, seq):
    ng = batch // SCAN_BATCH
    y = y[..., :LANES // 2].reshape(ng, seq, RWKV_HEAD, SCAN_BATCH, RWKV_HEADS).transpose(0, 3, 1, 4, 2)
    return y.reshape(batch * seq, RWKV_DIM)


def _state_to_scan_layout(s, batch):
    ng = batch // SCAN_BATCH
    s = s.reshape(ng, SCAN_BATCH, RWKV_HEADS, RWKV_HEAD, 2, HALF_HEAD).transpose(0, 5, 3, 4, 1, 2)
    return s.reshape(ng, HALF_HEAD, RWKV_HEAD, LANES)


def _state_from_scan_layout(s, batch):
    ng = batch // SCAN_BATCH
    s = s.reshape(ng, HALF_HEAD, RWKV_HEAD, 2, SCAN_BATCH, RWKV_HEADS).transpose(0, 4, 5, 2, 3, 1)
    return s.reshape(batch, RWKV_HEADS, RWKV_HEAD, RWKV_HEAD)


def _merge_kernel(h_ref, y_ref, bonus_ref, g_ref, oa_ref, gates_ref, lnw_ref, lnb_ref, ones_ref, wob_ref, woa_ref,
                  wout_ref, post_ref, o_ref):
    ones = ones_ref[...]
    y = y_ref[...]
    inv_n = 1.0 / RWKV_HEAD
    mu = _segsum(y, ones) * inv_n
    yc = y - mu
    var = _segsum(yc * yc, ones) * inv_n
    yn = yc * lax.rsqrt(var + GN_EPS) * lnw_ref[...] + lnb_ref[...] + bonus_ref[...]
    o_b = _dot((yn * g_ref[...]).astype(BF16), wob_ref[...])
    o_a = _dot(oa_ref[...].astype(BF16), woa_ref[...])
    d = o_ref.shape[1]
    gates = gates_ref[...]
    mix = _sigmoid(gates[:, :d]) * o_a + _sigmoid(gates[:, d:]) * o_b
    o_ref[...] = h_ref[...] + _rms(_dot(mix.astype(BF16), wout_ref[...]), post_ref[...])


def _merge(h, y, bonus, g, oa, gates, lnw, lnb, ones, wob, woa, wout, post_g):
    t, d = h.shape
    tm = _row_tile(t, 512)
    consts = (lnw, lnb, ones, wob, woa, wout, post_g)
    return pl.pallas_call(
        _merge_kernel,
        out_shape=jax.ShapeDtypeStruct((t, d), F32),
        grid=(t // tm,),
        in_specs=[_rows_spec(tm, d)] + [_rows_spec(tm, RWKV_DIM)] * 4 + [_rows_spec(tm, 2 * d)]
                 + [_const_spec(c.shape) for c in consts],
        out_specs=_rows_spec(tm, d),
        compiler_params=_cparams("parallel"),
        name="gated_merge_out_proj",
    )(h, y, bonus, g, oa, gates, *consts)


def _ple_kernel(h_ref, p_ref, pre_ref, wgate_ref, wproj_ref, post_ref, o_ref):
    h = h_ref[...]
    gp = _sigmoid(_dot(_rms(h, pre_ref[...]).astype(BF16), wgate_ref[...]))
    emb = _dot(p_ref[...].astype(BF16), wproj_ref[...])
    o_ref[...] = h + _rms(gp * emb, post_ref[...])


def _ple(h, p, pre_g, wgate, wproj, post_g):
    t, d = h.shape
    tm = _row_tile(t, 512)
    consts = (pre_g, wgate, wproj, post_g)
    return pl.pallas_call(
        _ple_kernel,
        out_shape=jax.ShapeDtypeStruct((t, d), F32),
        grid=(t // tm,),
        in_specs=[_rows_spec(tm, d), _rows_spec(tm, p.shape[1])] + [_const_spec(c.shape) for c in consts],
        out_specs=_rows_spec(tm, d),
        compiler_params=_cparams("parallel"),
        name="per_layer_embedding",
    )(h, p, *consts)


def _rope_tables(pos):
    inv = ROPE_THETA ** (-jnp.arange(0, ROPE_DIM, 2, dtype=F32) / ROPE_DIM)
    ang = pos[:, None] * inv[None, :]
    cos, sin = jnp.cos(ang), jnp.sin(ang)
    reps = ROPE_PAD // ROPE_DIM
    return jnp.tile(jnp.concatenate([cos, cos], axis=-1), (1, reps)), jnp.tile(jnp.concatenate([-sin, sin], axis=-1), (1, reps))


def _row(v):
    return v.reshape(1, -1).astype(F32)


def kernel(x_prompt, x_sample, p_prompt, p_sample, cache_ckv, cache_kpe, state_wkv, state_shift, page_table, ffn1_pre_g, ffn1_wg, ffn1_wu, ffn1_wd, ffn1_post_g, mix_pre_g, w_in, q_norm_g, w_uq, kv_norm_g, w_uk, w_uv, w_oa, rwkv_mu, rwkv_w0, rwkv_w_up, rwkv_a0, rwkv_a_up, rwkv_g_up, rwkv_k_k, rwkv_k_a, rwkv_r_k, rwkv_ln_w, rwkv_ln_b, w_ob, w_out, mix_post_g, ffn2_pre_g, ffn2_wg, ffn2_wu, ffn2_wd, ffn2_post_g, ple_pre_g, w_ple_gate, w_ple_proj, ple_post_g):
    depth = w_in.shape[0]
    assert depth == 1, "single-layer step"
    batch, seq, d_model = x_prompt.shape
    dec_batch, dec_seq, _ = x_sample.shape
    n_pages = page_table.shape[1]
    page = cache_ckv.shape[2]
    kv_rank = cache_ckv.shape[3]
    q_rank = w_uq.shape[1]
    shift_dim = state_shift.shape[2]
    tp, ts_tok = batch * seq, dec_batch * dec_seq
    assert batch % SCAN_BATCH == 0 and dec_batch % SCAN_BATCH == 0
    assert shift_dim == 3 * RWKV_DIM + W_LORA + A_LORA + G_LORA
    bf = lambda w: w.astype(BF16)

    w_in0 = w_in[0]
    c0, c1, c2, c3 = q_rank, q_rank + kv_rank, q_rank + kv_rank + ROPE_DIM, q_rank + kv_rank + ROPE_DIM + shift_dim
    wq, wkv, wrw, wgate = bf(w_in0[:, :c0]), bf(w_in0[:, c0:c1]), bf(w_in0[:, c2:c3]), bf(w_in0[:, c3:])
    wkpe = bf(jnp.pad(w_in0[:, c1:c2], ((0, 0), (0, ROPE_PAD - ROPE_DIM))))
    wuq3 = w_uq[0].reshape(q_rank, MLA_HEADS, NOPE_DIM + ROPE_DIM)
    wuq_nope = wuq3[:, :, :NOPE_DIM].reshape(q_rank, MLA_HEADS * NOPE_DIM)
    wuq_pe = jnp.pad(wuq3[:, :, NOPE_DIM:], ((0, 0), (0, 0), (0, ROPE_PAD - ROPE_DIM))).reshape(q_rank, MLA_HEADS * ROPE_PAD)
    wuq = bf(jnp.concatenate([wuq_nope, wuq_pe], axis=1))
    wukt = bf(w_uk[0].transpose(1, 2, 0))
    wuv = bf(w_uv[0].transpose(1, 0, 2))
    wuvt = bf(w_uv[0].transpose(1, 2, 0))
    wa_up = jnp.zeros((W_LORA + A_LORA, 2 * RWKV_DIM), F32)
    wa_up = bf(wa_up.at[:W_LORA, :RWKV_DIM].set(rwkv_w_up[0]).at[W_LORA:, RWKV_DIM:].set(rwkv_a_up[0]))
    seg = jnp.arange(RWKV_DIM) // RWKV_HEAD
    ones = (seg[:, None] == seg[None, :]).astype(BF16)
    ffn1 = (_row(ffn1_pre_g), bf(ffn1_wg[0]), bf(ffn1_wu[0]), bf(ffn1_wd[0]), _row(ffn1_post_g))
    ffn2 = (_row(ffn2_pre_g), bf(ffn2_wg[0]), bf(ffn2_wu[0]), bf(ffn2_wd[0]), _row(ffn2_post_g))
    inproj_w = (_row(mix_pre_g), wq, wkv, wkpe, wrw, wgate, _row(q_norm_g), wuq, _row(kv_norm_g), wukt)
    prep_w = (_row(rwkv_mu), _row(rwkv_w0), _row(rwkv_a0), wa_up, bf(rwkv_g_up[0]), _row(rwkv_k_k), _row(rwkv_k_a),
              _row(rwkv_r_k), ones)
    merge_w = (_row(rwkv_ln_w), _row(rwkv_ln_b), ones, bf(w_ob[0]), bf(w_oa[0]), bf(w_out[0]), _row(mix_post_g))
    ple_w = (_row(ple_pre_g), bf(w_ple_gate[0]), bf(w_ple_proj[0]), _row(ple_post_g))

    def rwkv_mix(rw, shift0, s0, nb, ns):
        r, w, k, v, a, b, g, bonus = _rwkv_prep(rw, shift0, ns, *prep_w)
        halves = [_to_scan_layout(t_, nb, ns, False) for t_ in (r, w, k, a, b)]
        y, s_fin = _wkv_scan(*halves, _to_scan_layout(v, nb, ns, True), _state_to_scan_layout(s0, nb))
        return _from_scan_layout(y, nb, ns), bonus, g, _state_from_scan_layout(s_fin, nb)

    tq = _row_tile(seq, 256)
    cos_p, sin_p = _rope_tables(jnp.arange(seq, dtype=F32))
    h1 = _ffn(x_prompt.reshape(tp, d_model), *ffn1)
    qcat, ckv_p, kpe_p, rw_p, gates, kcat, ckvt = _inproj(h1, *inproj_w, cos_p, sin_p, tq)
    oa = _attn_prompt(qcat, kcat, ckvt, wuvt, batch, seq, tq)
    y, bonus, g, wkv_p = rwkv_mix(rw_p, jnp.zeros((batch, shift_dim), F32),
                                  jnp.zeros((batch, RWKV_HEADS, RWKV_HEAD, RWKV_HEAD), F32), batch, seq)
    h2 = _merge(h1, y, bonus, g, oa, gates, *merge_w)
    h3 = _ffn(h2, *ffn2)
    y_prompt = _ple(h3, p_prompt[0].reshape(tp, -1), *ple_w)

    pos_s = jnp.tile(n_pages * page + jnp.arange(dec_seq, dtype=F32), dec_batch)
    cos_s, sin_s = _rope_tables(pos_s)
    h1 = _ffn(x_sample.reshape(ts_tok, d_model), *ffn1)
    qcat, ckv_s, kpe_s, rw_s, gates = _inproj(h1, *inproj_w, cos_s, sin_s, 0)
    qw = kv_rank + ROPE_PAD
    q3 = qcat.reshape(dec_batch, dec_seq, MLA_HEADS, qw).transpose(0, 2, 1, 3).reshape(
        dec_batch, MLA_HEADS * dec_seq, qw).astype(F32)
    o3 = _attn_paged(page_table, q3, ckv_s.reshape(dec_batch, dec_seq, kv_rank),
                     kpe_s.reshape(dec_batch, dec_seq, ROPE_DIM), cache_ckv, cache_kpe)
    o_lat = o3.reshape(dec_batch, MLA_HEADS, dec_seq, kv_rank).transpose(0, 2, 1, 3).reshape(ts_tok, MLA_HEADS * kv_rank)
    oa = _uv_proj(o_lat, wuv)
    y, bonus, g, wkv_s = rwkv_mix(rw_s, state_shift[0], state_wkv[0].astype(F32), dec_batch, dec_seq)
    h2 = _merge(h1, y, bonus, g, oa, gates, *merge_w)
    h3 = _ffn(h2, *ffn2)
    y_sample = _ple(h3, p_sample[0].reshape(ts_tok, -1), *ple_w)

    return (y_prompt.reshape(batch, seq, d_model), y_sample.reshape(dec_batch, dec_seq, d_model),
            ckv_p.reshape(1, batch, seq, kv_rank), kpe_p.reshape(1, batch, seq, ROPE_DIM),
            wkv_p[None], rw_p.reshape(batch, seq, shift_dim)[:, -1][None],
            ckv_s.reshape(1, dec_batch, dec_seq, kv_rank), kpe_s.reshape(1, dec_batch, dec_seq, ROPE_DIM),
            wkv_s[None], rw_s.reshape(dec_batch, dec_seq, shift_dim)[:, -1][None])
```

```python
import functools
import math

import jax
import jax.numpy as jnp
from jax import lax
from jax.experimental import pallas as pl
from jax.experimental.pallas import tpu as pltpu

F32 = jnp.float32
BF16 = jnp.bfloat16

MLA_HEADS = 8
NOPE_DIM = 64
ROPE_DIM = 32
V_DIM = 64
ROPE_THETA = 10000.0
SOFTMAX_SCALE = (NOPE_DIM + ROPE_DIM) ** -0.5
Q_SCALE = SOFTMAX_SCALE * math.log2(math.e)
RWKV_HEADS = 8
RWKV_HEAD = 64
RWKV_DIM = RWKV_HEADS * RWKV_HEAD
W_LORA = 64
A_LORA = 64
G_LORA = 128
GN_EPS = 64e-5
NORM_EPS = 1e-6

LANES = 128
VMEM_LIMIT_BYTES = 56 * 1024 * 1024

ROPE_PAD = LANES
SCAN_BATCH = 8
HALF_HEAD = RWKV_HEAD // 2
MAX_PAGES_PER_STREAM = 16
PAGED_STREAMS = 2


def _cparams(*sem):
    return pltpu.CompilerParams(dimension_semantics=sem, vmem_limit_bytes=VMEM_LIMIT_BYTES)


def _dot(a, b):
    return jnp.dot(a, b, preferred_element_type=F32)


def _dot_nt(a, b):
    return lax.dot_general(a, b, (((1,), (1,)), ((), ())), preferred_element_type=F32)


def _rms(x, g):
    return x * lax.rsqrt(jnp.mean(x * x, axis=-1, keepdims=True) + NORM_EPS) * g


def _sigmoid(x):
    return 1.0 / (1.0 + jnp.exp(-x))


def _rope(x, cos, sin_signed):
    lane = lax.broadcasted_iota(jnp.int32, x.shape, 1)
    swapped = jnp.where((lane % ROPE_DIM) < ROPE_DIM // 2,
                        pltpu.roll(x, LANES - ROPE_DIM // 2, 1), pltpu.roll(x, ROPE_DIM // 2, 1))
    return x * cos + swapped * sin_signed


def _segsum(x, ones_bf16):
    hi = x.astype(BF16)
    lo = (x - hi.astype(F32)).astype(BF16)
    return _dot(hi, ones_bf16) + _dot(lo, ones_bf16)


def _row_tile(n, cap):
    t = cap
    while n % t:
        t //= 2
    assert t >= 8, (n, cap)
    return t


def _const_spec(shape):
    nd = len(shape)
    return pl.BlockSpec(shape, lambda *_: (0,) * nd)


def _rows_spec(tm, width):
    return pl.BlockSpec((tm, width), lambda i: (i, 0))


def _ffn_kernel(x_ref, pre_ref, wg_ref, wu_ref, wd_ref, post_ref, o_ref, *, n_chunks):
    x = x_ref[...]
    xn = _rms(x, pre_ref[...]).astype(BF16)
    fc = wg_ref.shape[1] // n_chunks
    acc = None
    for c in range(n_chunks):
        g = _dot(xn, wg_ref[:, c * fc:(c + 1) * fc])
        u = _dot(xn, wu_ref[:, c * fc:(c + 1) * fc])
        a = (g * _sigmoid(g) * u).astype(BF16)
        d = _dot(a, wd_ref[c * fc:(c + 1) * fc, :])
        acc = d if acc is None else acc + d
    o_ref[...] = x + 0.5 * _rms(acc, post_ref[...])


def _ffn(x, pre_g, wg, wu, wd, post_g):
    t, d = x.shape
    f = wg.shape[1]
    tm = _row_tile(t, 512)
    n_chunks = 2 if f % (2 * LANES) == 0 else 1
    return pl.pallas_call(
        functools.partial(_ffn_kernel, n_chunks=n_chunks),
        out_shape=jax.ShapeDtypeStruct((t, d), F32),
        grid=(t // tm,),
        in_specs=[_rows_spec(tm, d), _const_spec((1, d)), _const_spec((d, f)), _const_spec((d, f)),
                  _const_spec((f, d)), _const_spec((1, d))],
        out_specs=_rows_spec(tm, d),
        compiler_params=_cparams("parallel"),
        name="ffn_half_step",
    )(x, pre_g, wg, wu, wd, post_g)


def _inproj_kernel(h_ref, g_ref, wq_ref, wkv_ref, wkpe_ref, wrw_ref, wgate_ref, qng_ref, wuq_ref, kvg_ref,
                   wukt_ref, cos_ref, sin_ref, qcat_ref, ckv_ref, kpe_ref, rw_ref, gates_ref, *kt_refs, tk):
    u = _rms(h_ref[...], g_ref[...]).astype(BF16)
    rw_ref[...] = _dot(u, wrw_ref[...])
    gates_ref[...] = _dot(u, wgate_ref[...])
    ckv = _rms(_dot(u, wkv_ref[...]), kvg_ref[...])
    ckv_ref[...] = ckv
    cos = cos_ref[...]
    sin = sin_ref[...]
    kpe = _rope(_dot(u, wkpe_ref[...]), cos, sin)
    kpe_ref[...] = kpe[:, :ROPE_DIM]
    kv_rank = ckv.shape[1]
    if kt_refs:
        kcat_ref, ckvt_ref = kt_refs
        kcat_ref[:, :kv_rank] = ckv.astype(BF16)
        kcat_ref[:, kv_rank:] = kpe.astype(BF16)
        for j in range(ckv.shape[0] // tk):
            ckvt_ref[j] = ckv[j * tk:(j + 1) * tk, :].T.astype(BF16)
    cq = _rms(_dot(u, wq_ref[...]), qng_ref[...]).astype(BF16)
    q = _dot(cq, wuq_ref[...])
    nope_w = MLA_HEADS * NOPE_DIM
    qw = kv_rank + ROPE_PAD
    for h in range(MLA_HEADS):
        nope = q[:, h * NOPE_DIM:(h + 1) * NOPE_DIM].astype(BF16)
        qcat_ref[:, h * qw:h * qw + kv_rank] = (_dot(nope, wukt_ref[h]) * Q_SCALE).astype(BF16)
        pe = q[:, nope_w + h * ROPE_PAD: nope_w + (h + 1) * ROPE_PAD]
        qcat_ref[:, h * qw + kv_rank:(h + 1) * qw] = (_rope(pe, cos, sin) * Q_SCALE).astype(BF16)


def _inproj(h, g, wq, wkv, wkpe, wrw, wgate, qng, wuq, kvg, wukt, cos, sin, tk):
    t, d = h.shape
    tm = _row_tile(t, 512)
    kv_rank = wkv.shape[1]
    qw = kv_rank + ROPE_PAD
    n_pos = cos.shape[0] // tm
    out_shape = [jax.ShapeDtypeStruct((t, MLA_HEADS * qw), BF16), jax.ShapeDtypeStruct((t, kv_rank), F32),
                 jax.ShapeDtypeStruct((t, ROPE_DIM), F32), jax.ShapeDtypeStruct((t, wrw.shape[1]), F32),
                 jax.ShapeDtypeStruct((t, wgate.shape[1]), F32)]
    out_specs = [_rows_spec(tm, s.shape[1]) for s in out_shape]
    if tk:
        assert tm % tk == 0
        out_shape += [jax.ShapeDtypeStruct((t, qw), BF16), jax.ShapeDtypeStruct((t // tk, kv_rank, tk), BF16)]
        out_specs += [_rows_spec(tm, qw), pl.BlockSpec((tm // tk, kv_rank, tk), lambda i: (i, 0, 0))]
    consts = (g, wq, wkv, wkpe, wrw, wgate, qng, wuq, kvg, wukt)
    pos_spec = pl.BlockSpec((tm, ROPE_PAD), lambda i: (i % n_pos, 0))
    return pl.pallas_call(
        functools.partial(_inproj_kernel, tk=tk),
        out_shape=out_shape,
        grid=(t // tm,),
        in_specs=[_rows_spec(tm, d)] + [_const_spec(c.shape) for c in consts] + [pos_spec, pos_spec],
        out_specs=out_specs,
        compiler_params=_cparams("parallel"),
        name="mixer_in_proj",
    )(h, *consts, cos, sin)


def _attn_kernel(qcat_ref, kcat_ref, ckvt_ref, wuvt_ref, o_ref, m_scr, l_scr, acc_scr, *, tq):
    i = pl.program_id(1)
    qw = kcat_ref.shape[1]
    m_scr[...] = jnp.full(m_scr.shape, -jnp.inf, F32)
    l_scr[...] = jnp.zeros(l_scr.shape, F32)
    acc_scr[...] = jnp.zeros(acc_scr.shape, F32)

    def block(kb, diagonal):
        off = pl.multiple_of(kb * tq, tq)
        kcat = kcat_ref[pl.ds(off, tq), :]
        vt = ckvt_ref[kb]
        def scores(h):
            return _dot_nt(kcat, qcat_ref[:, h * qw:(h + 1) * qw])

        s_next = scores(0)
        for h in range(MLA_HEADS):
            s, s_next = s_next, (scores(h + 1) if h + 1 < MLA_HEADS else None)
            if diagonal:
                key = lax.broadcasted_iota(jnp.int32, s.shape, 0)
                qry = lax.broadcasted_iota(jnp.int32, s.shape, 1)
                s = jnp.where(key <= qry, s, -jnp.inf)
            m_prev = m_scr[h]
            m_new = jnp.maximum(m_prev, jnp.max(s, axis=0, keepdims=True))
            alpha = jnp.exp2(m_prev - m_new)
            p = jnp.exp2(s - m_new)
            l_scr[h] = alpha * l_scr[h] + jnp.sum(p, axis=0, keepdims=True)
            acc_scr[h] = alpha * acc_scr[h] + _dot(vt, p.astype(BF16))
            m_scr[h] = m_new

    def off_diagonal(kb, carry):
        block(kb, False)
        return carry

    lax.fori_loop(0, i, off_diagonal, 0)
    block(i, True)
    outs = [_dot(wuvt_ref[h], (acc_scr[h] / l_scr[h]).astype(BF16)) for h in range(MLA_HEADS)]
    o_ref[...] = jnp.concatenate(outs, axis=0).T


def _attn_prompt(qcat, kcat, ckvt, wuvt, batch, seq, tq):
    qw = kcat.shape[1]
    kv_rank = ckvt.shape[1]
    nq = seq // tq
    return pl.pallas_call(
        functools.partial(_attn_kernel, tq=tq),
        out_shape=jax.ShapeDtypeStruct((batch * seq, MLA_HEADS * V_DIM), F32),
        grid=(batch, nq),
        in_specs=[pl.BlockSpec((tq, MLA_HEADS * qw), lambda b, i: (b * nq + i, 0)),
                  pl.BlockSpec((seq, qw), lambda b, i: (b, 0)),
                  pl.BlockSpec((nq, kv_rank, tq), lambda b, i: (b, 0, 0)),
                  _const_spec(wuvt.shape)],
        out_specs=pl.BlockSpec((tq, MLA_HEADS * V_DIM), lambda b, i: (b * nq + i, 0)),
        scratch_shapes=[pltpu.VMEM((MLA_HEADS, 1, tq), F32), pltpu.VMEM((MLA_HEADS, 1, tq), F32),
                        pltpu.VMEM((MLA_HEADS, kv_rank, tq), F32)],
        compiler_params=_cparams("parallel", "arbitrary"),
        name="mla_prompt_attention",
    )(qcat, kcat, ckvt, wuvt)


def _paged_kernel(pt_ref, q_ref, knew_ref, kpnew_ref, cache_ckv, cache_kpe_t, o_ref,
                  kc_buf, kp_buf, sem, *, n_batch, chunks_per_batch, pages_per_stream, page, dec_seq):
    total = n_batch * chunks_per_batch
    pages_per_chunk = PAGED_STREAMS * pages_per_stream
    kv_rank = kc_buf.shape[2]

    def page_copies(g, slot, p):
        pg = pt_ref[g * pages_per_chunk + p]
        rows = pl.ds(p * page, page)
        return (pltpu.make_async_copy(cache_ckv.at[0, pg], kc_buf.at[slot, rows, :], sem.at[0, slot]),
                pltpu.make_async_copy(cache_kpe_t.at[0, pg], kp_buf.at[slot, :, rows], sem.at[1, slot]))

    def start_chunk(g, slot):
        for p in range(pages_per_chunk):
            for cp in page_copies(g, slot, p):
                cp.start()

    def wait_chunk(g, slot):
        for p in range(pages_per_chunk):
            for cp in page_copies(g, slot, p):
                cp.wait()

    start_chunk(0, 0)

    def batch_body(b, carry):
        qf = q_ref[b]
        qlf = qf[:, :kv_rank]
        qpf = qf[:, kv_rank:kv_rank + ROPE_DIM]
        qb = qlf.astype(BF16)
        qpb = qpf.astype(BF16)
        rows = qf.shape[0]
        stream_rows = pages_per_stream * page

        def chunk_body(c, states):
            g = b * chunks_per_batch + c
            slot = g % 2

            @pl.when(g + 1 < total)
            def _():
                start_chunk(g + 1, 1 - slot)

            wait_chunk(g, slot)
            keys, scores = [], []
            for st in range(PAGED_STREAMS):
                kc = kc_buf[slot, st * stream_rows:(st + 1) * stream_rows, :].astype(BF16)
                kp_t = kp_buf[slot, :, st * stream_rows:(st + 1) * stream_rows].astype(BF16)
                keys.append(kc)
                scores.append(_dot_nt(qb, kc) + _dot(qpb, kp_t))
            new_states = []
            for (m_prev, l_prev, acc), kc, s in zip(states, keys, scores):
                m_new = jnp.maximum(m_prev, jnp.max(s, axis=-1, keepdims=True))
                alpha = jnp.exp2(m_prev - m_new)
                p = jnp.exp2(s - m_new)
                l_new = alpha * l_prev + jnp.sum(p, axis=-1, keepdims=True)
                new_states.append((m_new, l_new, alpha * acc + _dot(p.astype(BF16), kc)))
            return tuple(new_states)

        init = tuple((jnp.full((rows, 1), -jnp.inf, F32), jnp.zeros((rows, 1), F32), jnp.zeros(qlf.shape, F32))
                     for _ in range(PAGED_STREAMS))
        states = lax.fori_loop(0, chunks_per_batch, chunk_body, init)

        knew = knew_ref[b]
        kpnew = kpnew_ref[b]
        tok = lax.broadcasted_iota(jnp.int32, (rows, 1), 0) % dec_seq
        s_new = []
        for j in range(dec_seq):
            sj = (jnp.sum(qlf * knew[j:j + 1, :], axis=-1, keepdims=True)
                  + jnp.sum(qpf * kpnew[j:j + 1, :], axis=-1, keepdims=True))
            s_new.append(jnp.where(tok >= j, sj, -jnp.inf))
        m_new = states[0][0]
        for m_st, _, _ in states[1:]:
            m_new = jnp.maximum(m_new, m_st)
        for sj in s_new:
            m_new = jnp.maximum(m_new, sj)
        l_new = jnp.zeros((rows, 1), F32)
        acc = jnp.zeros(qlf.shape, F32)
        for m_st, l_st, acc_st in states:
            alpha = jnp.exp2(m_st - m_new)
            l_new = l_new + alpha * l_st
            acc = acc + alpha * acc_st
        for j, sj in enumerate(s_new):
            pj = jnp.exp2(sj - m_new)
            l_new = l_new + pj
            acc = acc + pj * knew[j:j + 1, :]
        o_ref[b] = acc / l_new
        return carry

    lax.fori_loop(0, n_batch, batch_body, 0)


def _attn_paged(page_table, q3, knew3, kpnew3, cache_ckv, cache_kpe_t):
    n_batch, n_pages = page_table.shape
    page = cache_ckv.shape[2]
    kv_rank = cache_ckv.shape[3]
    dec_seq = knew3.shape[1]
    pps = MAX_PAGES_PER_STREAM
    while n_pages % (PAGED_STREAMS * pps):
        pps //= 2
    assert pps >= 1, n_pages
    ppc = PAGED_STREAMS * pps
    kernel = functools.partial(_paged_kernel, n_batch=n_batch, chunks_per_batch=n_pages // ppc,
                               pages_per_stream=pps, page=page, dec_seq=dec_seq)
    vmem = lambda shape: pl.BlockSpec(shape, lambda i, pt: (0,) * len(shape))
    any_spec = pl.BlockSpec(memory_space=pl.ANY)
    out_shape = (n_batch, q3.shape[1], kv_rank)
    return pl.pallas_call(
        kernel,
        out_shape=jax.ShapeDtypeStruct(out_shape, F32),
        grid_spec=pltpu.PrefetchScalarGridSpec(
            num_scalar_prefetch=1,
            grid=(1,),
            in_specs=[vmem(q3.shape), vmem(knew3.shape), vmem(kpnew3.shape), any_spec, any_spec],
            out_specs=vmem(out_shape),
            scratch_shapes=[pltpu.VMEM((2, ppc * page, kv_rank), F32),
                            pltpu.VMEM((2, cache_kpe_t.shape[2], ppc * page), F32),
                            pltpu.SemaphoreType.DMA((2, 2))]),
        compiler_params=_cparams("arbitrary"),
        name="mla_paged_attention",
    )(page_table.reshape(-1), q3, knew3, kpnew3, cache_ckv, cache_kpe_t)


def _uv_kernel(o_ref, wuv_ref, out_ref):
    kv_rank = wuv_ref.shape[1]
    outs = [_dot(o_ref[:, h * kv_rank:(h + 1) * kv_rank].astype(BF16), wuv_ref[h]) for h in range(MLA_HEADS)]
    out_ref[...] = jnp.concatenate(outs, axis=-1)


def _uv_proj(o_lat, wuv):
    t = o_lat.shape[0]
    return pl.pallas_call(
        _uv_kernel,
        out_shape=jax.ShapeDtypeStruct((t, MLA_HEADS * V_DIM), F32),
        grid=(1,),
        in_specs=[_const_spec(o_lat.shape), _const_spec(wuv.shape)],
        out_specs=_const_spec((t, MLA_HEADS * V_DIM)),
        compiler_params=_cparams("arbitrary"),
        name="mla_value_up_proj",
    )(o_lat, wuv)


def _softplus(z):
    return jnp.maximum(z, 0.0) + jnp.log(1.0 + jnp.exp(-jnp.abs(z)))


def _rwkv_prep_kernel(rw_ref, first_ref, *refs, seq, tiles_per_seq):
    x = rw_ref[...]
    row = lax.broadcasted_iota(jnp.int32, x.shape, 0)
    rolled = pltpu.roll(x, 1, 0)
    if tiles_per_seq:
        prev8_ref, refs = refs[0], refs[1:]
        at_start = pl.program_id(0) % tiles_per_seq == 0
        first = jnp.where(at_start, first_ref[...], prev8_ref[7:8, :])
        prev = jnp.where(row == 0, first, rolled)
    else:
        prev = jnp.where(row % seq == 0, first_ref[...], rolled)
    (mu_ref, w0_ref, a0_ref, wa_up_ref, g_up_ref, kk_ref, ka_ref, rk_ref, ones_ref,
     r_o, w_o, k_o, v_o, a_o, b_o, g_o, bonus_o) = refs
    xr = x + (prev - x) * mu_ref[...]
    d = RWKV_DIM
    r = xr[:, 0:d]
    k = xr[:, d:2 * d]
    v = xr[:, 2 * d:3 * d]
    xwa = xr[:, 3 * d:3 * d + W_LORA + A_LORA]
    xg = xr[:, 3 * d + W_LORA + A_LORA:]
    lane = lax.broadcasted_iota(jnp.int32, xwa.shape, 1)
    wa = _dot(jnp.where(lane < W_LORA, jnp.tanh(xwa), xwa).astype(BF16), wa_up_ref[...])
    w = -_softplus(-(w0_ref[...] + wa[:, :d])) - 0.5
    a = _sigmoid(a0_ref[...] + wa[:, d:])
    ones = ones_ref[...]
    kk = k * kk_ref[...]
    kk = kk / jnp.maximum(jnp.sqrt(_segsum(kk * kk, ones)), 1e-12)
    k2 = k * (1.0 + (a - 1.0) * ka_ref[...])
    r_o[...] = r
    w_o[...] = jnp.exp(-jnp.exp(w))
    k_o[...] = k2
    v_o[...] = v
    a_o[...] = -kk
    b_o[...] = kk * a
    g_o[...] = _dot(_sigmoid(xg).astype(BF16), g_up_ref[...])
    bonus_o[...] = _segsum(r * k2 * rk_ref[...], ones) * v


def _rwkv_prep(rw, shift0, seq, mu, w0, a0, wa_up, g_up, k_k, k_a, r_k, ones):
    t, width = rw.shape
    tm = _row_tile(t, 512)
    consts = (mu, w0, a0, wa_up, g_up, k_k, k_a, r_k, ones)
    if seq % tm == 0:
        tiles_per_seq = seq // tm
        first = shift0.reshape(-1, 1, width)
        lead = [first, rw]
        lead_specs = [pl.BlockSpec((None, 1, width), lambda i: (i // tiles_per_seq, 0, 0)),
                      pl.BlockSpec((8, width), lambda i: (jnp.maximum(i * (tm // 8) - 1, 0), 0))]
    else:
        assert tm % seq == 0
        tiles_per_seq = 0
        lead = [jnp.repeat(shift0, seq, axis=0)]
        lead_specs = [_rows_spec(tm, width)]
    return pl.pallas_call(
        functools.partial(_rwkv_prep_kernel, seq=seq, tiles_per_seq=tiles_per_seq),
        out_shape=[jax.ShapeDtypeStruct((t, RWKV_DIM), F32)] * 8,
        grid=(t // tm,),
        in_specs=[_rows_spec(tm, width)] + lead_specs + [_const_spec(c.shape) for c in consts],
        out_specs=[_rows_spec(tm, RWKV_DIM)] * 8,
        compiler_params=_cparams("parallel"),
        name="rwkv_prep",
    )(rw, *lead, *consts)


def _scan_kernel(r_ref, w_ref, k_ref, a_ref, b_ref, v_ref, s0_ref, y_ref, sfin_ref, st_ref, *, ts):
    j = pl.program_id(1)

    @pl.when(j == 0)
    def _():
        st_ref[...] = s0_ref[...]

    def row(ref, t, kq):
        return ref[t, pl.ds(kq, 1), :]

    def state_dot(ref, t):
        acc = st_ref[0] * row(ref, t, 0)
        for kq in range(1, HALF_HEAD):
            acc = acc + st_ref[kq] * row(ref, t, kq)
        return acc

    def step(t, sa_half):
        sa = sa_half + pltpu.roll(sa_half, LANES // 2, 1)
        vt = v_ref[t]
        t_next = jnp.minimum(t + 1, ts - 1)
        y = None
        sa_next = None
        for kq in range(HALF_HEAD):
            s_new = st_ref[kq] * row(w_ref, t, kq) + sa * row(b_ref, t, kq) + vt * row(k_ref, t, kq)
            st_ref[kq] = s_new
            y_term = s_new * row(r_ref, t, kq)
            a_term = s_new * row(a_ref, t_next, kq)
            y = y_term if y is None else y + y_term
            sa_next = a_term if sa_next is None else sa_next + a_term
        y_ref[t] = y + pltpu.roll(y, LANES // 2, 1)
        return sa_next

    lax.fori_loop(0, ts, step, state_dot(a_ref, 0))

    @pl.when(j == pl.num_programs(1) - 1)
    def _():
        sfin_ref[...] = st_ref[...]


def _wkv_scan(r, w, k, a, b, v, s0):
    ng, seq = r.shape[:2]
    ts = _row_tile(seq, 128) if seq % 8 == 0 else seq
    half_spec = pl.BlockSpec((None, ts, HALF_HEAD, LANES), lambda g, j: (g, j, 0, 0))
    full_spec = pl.BlockSpec((None, ts, RWKV_HEAD, LANES), lambda g, j: (g, j, 0, 0))
    state_spec = pl.BlockSpec((None, HALF_HEAD, RWKV_HEAD, LANES), lambda g, j: (g, 0, 0, 0))
    return pl.pallas_call(
        functools.partial(_scan_kernel, ts=ts),
        out_shape=[jax.ShapeDtypeStruct(v.shape, F32), jax.ShapeDtypeStruct(s0.shape, F32)],
        grid=(ng, seq // ts),
        in_specs=[half_spec] * 5 + [full_spec, state_spec],
        out_specs=[full_spec, state_spec],
        scratch_shapes=[pltpu.VMEM((HALF_HEAD, RWKV_HEAD, LANES), F32)],
        compiler_params=_cparams("parallel", "arbitrary"),
        name="rwkv7_state_scan",
    )(r, w, k, a, b, v, s0)


def _to_scan_layout(x, batch, seq, dup):
    ng = batch // SCAN_BATCH
    if dup:
        x = x.reshape(ng, SCAN_BATCH, seq, RWKV_HEADS, RWKV_HEAD).transpose(0, 2, 4, 1, 3)
        x = x.reshape(ng, seq, RWKV_HEAD, SCAN_BATCH * RWKV_HEADS)
        return jnp.concatenate([x, x], axis=-1)
    x = x.reshape(ng, SCAN_BATCH, seq, RWKV_HEADS, 2, HALF_HEAD).transpose(0, 2, 5, 4, 1, 3)
    return x.reshape(ng, seq, HALF_HEAD, LANES)


def _from_scan_layout(y, batch, seq):
    ng = batch // SCAN_BATCH
    y = y[..., :LANES // 2].reshape(ng, seq, RWKV_HEAD, SCAN_BATCH, RWKV_HEADS).transpose(0, 3, 1, 4, 2)
    return y.reshape(batch * seq, RWKV_DIM)


def _state_to_scan_layout(s, batch):
    ng = batch // SCAN_BATCH
    s = s.reshape(ng, SCAN_BATCH, RWKV_HEADS, RWKV_HEAD, 2, HALF_HEAD).transpose(0, 5, 3, 4, 1, 2)
    return s.reshape(ng, HALF_HEAD, RWKV_HEAD, LANES)


def _state_from_scan_layout(s, batch):
    ng = batch // SCAN_BATCH
    s = s.reshape(ng, HALF_HEAD, RWKV_HEAD, 2, SCAN_BATCH, RWKV_HEADS).transpose(0, 4, 5, 2, 3, 1)
    return s.reshape(batch, RWKV_HEADS, RWKV_HEAD, RWKV_HEAD)


def _merge_kernel(h_ref, y_ref, bonus_ref, g_ref, oa_ref, gates_ref, lnw_ref, lnb_ref, ones_ref, wob_ref, woa_ref,
                  wout_ref, post_ref, o_ref):
    ones = ones_ref[...]
    y = y_ref[...]
    inv_n = 1.0 / RWKV_HEAD
    mu = _segsum(y, ones) * inv_n
    yc = y - mu
    var = _segsum(yc * yc, ones) * inv_n
    yn = yc * lax.rsqrt(var + GN_EPS) * lnw_ref[...] + lnb_ref[...] + bonus_ref[...]
    o_b = _dot((yn * g_ref[...]).astype(BF16), wob_ref[...])
    o_a = _dot(oa_ref[...].astype(BF16), woa_ref[...])
    d = o_ref.shape[1]
    gates = gates_ref[...]
    mix = _sigmoid(gates[:, :d]) * o_a + _sigmoid(gates[:, d:]) * o_b
    o_ref[...] = h_ref[...] + _rms(_dot(mix.astype(BF16), wout_ref[...]), post_ref[...])


def _merge(h, y, bonus, g, oa, gates, lnw, lnb, ones, wob, woa, wout, post_g):
    t, d = h.shape
    tm = _row_tile(t, 512)
    consts = (lnw, lnb, ones, wob, woa, wout, post_g)
    return pl.pallas_call(
        _merge_kernel,
        out_shape=jax.ShapeDtypeStruct((t, d), F32),
        grid=(t // tm,),
        in_specs=[_rows_spec(tm, d)] + [_rows_spec(tm, RWKV_DIM)] * 4 + [_rows_spec(tm, 2 * d)]
                 + [_const_spec(c.shape) for c in consts],
        out_specs=_rows_spec(tm, d),
        compiler_params=_cparams("parallel"),
        name="gated_merge_out_proj",
    )(h, y, bonus, g, oa, gates, *consts)


def _ple_kernel(h_ref, p_ref, pre_ref, wgate_ref, wproj_ref, post_ref, o_ref):
    h = h_ref[...]
    gp = _sigmoid(_dot(_rms(h, pre_ref[...]).astype(BF16), wgate_ref[...]))
    emb = _dot(p_ref[...].astype(BF16), wproj_ref[...])
    o_ref[...] = h + _rms(gp * emb, post_ref[...])


def _ple(h, p, pre_g, wgate, wproj, post_g):
    t, d = h.shape
    tm = _row_tile(t, 512)
    consts = (pre_g, wgate, wproj, post_g)
    return pl.pallas_call(
        _ple_kernel,
        out_shape=jax.ShapeDtypeStruct((t, d), F32),
        grid=(t // tm,),
        in_specs=[_rows_spec(tm, d), _rows_spec(tm, p.shape[1])] + [_const_spec(c.shape) for c in consts],
        out_specs=_rows_spec(tm, d),
        compiler_params=_cparams("parallel"),
        name="per_layer_embedding",
    )(h, p, *consts)


def _rope_tables(pos):
    inv = ROPE_THETA ** (-jnp.arange(0, ROPE_DIM, 2, dtype=F32) / ROPE_DIM)
    ang = pos[:, None] * inv[None, :]
    cos, sin = jnp.cos(ang), jnp.sin(ang)
    reps = ROPE_PAD // ROPE_DIM
    return jnp.tile(jnp.concatenate([cos, cos], axis=-1), (1, reps)), jnp.tile(jnp.concatenate([-sin, sin], axis=-1), (1, reps))


def _row(v):
    return v.reshape(1, -1).astype(F32)


def kernel(x_prompt, x_sample, p_prompt, p_sample, cache_ckv, cache_kpe, state_wkv, state_shift, page_table, ffn1_pre_g, ffn1_wg, ffn1_wu, ffn1_wd, ffn1_post_g, mix_pre_g, w_in, q_norm_g, w_uq, kv_norm_g, w_uk, w_uv, w_oa, rwkv_mu, rwkv_w0, rwkv_w_up, rwkv_a0, rwkv_a_up, rwkv_g_up, rwkv_k_k, rwkv_k_a, rwkv_r_k, rwkv_ln_w, rwkv_ln_b, w_ob, w_out, mix_post_g, ffn2_pre_g, ffn2_wg, ffn2_wu, ffn2_wd, ffn2_post_g, ple_pre_g, w_ple_gate, w_ple_proj, ple_post_g):
    depth = w_in.shape[0]
    assert depth == 1, "single-layer step"
    batch, seq, d_model = x_prompt.shape
    dec_batch, dec_seq, _ = x_sample.shape
    n_pages = page_table.shape[1]
    page = cache_ckv.shape[2]
    kv_rank = cache_ckv.shape[3]
    q_rank = w_uq.shape[1]
    shift_dim = state_shift.shape[2]
    tp, ts_tok = batch * seq, dec_batch * dec_seq
    assert batch % SCAN_BATCH == 0 and dec_batch % SCAN_BATCH == 0
    assert shift_dim == 3 * RWKV_DIM + W_LORA + A_LORA + G_LORA
    bf = lambda w: w.astype(BF16)

    w_in0 = w_in[0]
    c0, c1, c2, c3 = q_rank, q_rank + kv_rank, q_rank + kv_rank + ROPE_DIM, q_rank + kv_rank + ROPE_DIM + shift_dim
    wq, wkv, wrw, wgate = bf(w_in0[:, :c0]), bf(w_in0[:, c0:c1]), bf(w_in0[:, c2:c3]), bf(w_in0[:, c3:])
    wkpe = bf(jnp.pad(w_in0[:, c1:c2], ((0, 0), (0, ROPE_PAD - ROPE_DIM))))
    wuq3 = w_uq[0].reshape(q_rank, MLA_HEADS, NOPE_DIM + ROPE_DIM)
    wuq_nope = wuq3[:, :, :NOPE_DIM].reshape(q_rank, MLA_HEADS * NOPE_DIM)
    wuq_pe = jnp.pad(wuq3[:, :, NOPE_DIM:], ((0, 0), (0, 0), (0, ROPE_PAD - ROPE_DIM))).reshape(q_rank, MLA_HEADS * ROPE_PAD)
    wuq = bf(jnp.concatenate([wuq_nope, wuq_pe], axis=1))
    wukt = bf(w_uk[0].transpose(1, 2, 0))
    wuv = bf(w_uv[0].transpose(1, 0, 2))
    wuvt = bf(w_uv[0].transpose(1, 2, 0))
    wa_up = jnp.zeros((W_LORA + A_LORA, 2 * RWKV_DIM), F32)
    wa_up = bf(wa_up.at[:W_LORA, :RWKV_DIM].set(rwkv_w_up[0]).at[W_LORA:, RWKV_DIM:].set(rwkv_a_up[0]))
    seg = jnp.arange(RWKV_DIM) // RWKV_HEAD
    ones = (seg[:, None] == seg[None, :]).astype(BF16)
    ffn1 = (_row(ffn1_pre_g), bf(ffn1_wg[0]), bf(ffn1_wu[0]), bf(ffn1_wd[0]), _row(ffn1_post_g))
    ffn2 = (_row(ffn2_pre_g), bf(ffn2_wg[0]), bf(ffn2_wu[0]), bf(ffn2_wd[0]), _row(ffn2_post_g))
    inproj_w = (_row(mix_pre_g), wq, wkv, wkpe, wrw, wgate, _row(q_norm_g), wuq, _row(kv_norm_g), wukt)
    prep_w = (_row(rwkv_mu), _row(rwkv_w0), _row(rwkv_a0), wa_up, bf(rwkv_g_up[0]), _row(rwkv_k_k), _row(rwkv_k_a),
              _row(rwkv_r_k), ones)
    merge_w = (_row(rwkv_ln_w), _row(rwkv_ln_b), ones, bf(w_ob[0]), bf(w_oa[0]), bf(w_out[0]), _row(mix_post_g))
    ple_w = (_row(ple_pre_g), bf(w_ple_gate[0]), bf(w_ple_proj[0]), _row(ple_post_g))

    def rwkv_mix(rw, shift0, s0, nb, ns):
        r, w, k, v, a, b, g, bonus = _rwkv_prep(rw, shift0, ns, *prep_w)
        halves = [_to_scan_layout(t_, nb, ns, False) for t_ in (r, w, k, a, b)]
        y, s_fin = _wkv_scan(*halves, _to_scan_layout(v, nb, ns, True), _state_to_scan_layout(s0, nb))
        return _from_scan_layout(y, nb, ns), bonus, g, _state_from_scan_layout(s_fin, nb)

    tq = _row_tile(seq, 256)
    cos_p, sin_p = _rope_tables(jnp.arange(seq, dtype=F32))
    h1 = _ffn(x_prompt.reshape(tp, d_model), *ffn1)
    qcat, ckv_p, kpe_p, rw_p, gates, kcat, ckvt = _inproj(h1, *inproj_w, cos_p, sin_p, tq)
    oa = _attn_prompt(qcat, kcat, ckvt, wuvt, batch, seq, tq)
    y, bonus, g, wkv_p = rwkv_mix(rw_p, jnp.zeros((batch, shift_dim), F32),
                                  jnp.zeros((batch, RWKV_HEADS, RWKV_HEAD, RWKV_HEAD), F32), batch, seq)
    h2 = _merge(h1, y, bonus, g, oa, gates, *merge_w)
    h3 = _ffn(h2, *ffn2)
    y_prompt = _ple(h3, p_prompt[0].reshape(tp, -1), *ple_w)

    pos_s = jnp.tile(n_pages * page + jnp.arange(dec_seq, dtype=F32), dec_batch)
    cos_s, sin_s = _rope_tables(pos_s)
    h1 = _ffn(x_sample.reshape(ts_tok, d_model), *ffn1)
    qcat, ckv_s, kpe_s, rw_s, gates = _inproj(h1, *inproj_w, cos_s, sin_s, 0)
    qw = kv_rank + ROPE_PAD
    q3 = qcat.reshape(dec_batch, dec_seq, MLA_HEADS, qw).transpose(0, 2, 1, 3).reshape(
        dec_batch, MLA_HEADS * dec_seq, qw).astype(F32)
    o3 = _attn_paged(page_table, q3, ckv_s.reshape(dec_batch, dec_seq, kv_rank),
                     kpe_s.reshape(dec_batch, dec_seq, ROPE_DIM), cache_ckv, jnp.swapaxes(cache_kpe, 2, 3))
    o_lat = o3.reshape(dec_batch, MLA_HEADS, dec_seq, kv_rank).transpose(0, 2, 1, 3).reshape(ts_tok, MLA_HEADS * kv_rank)
    oa = _uv_proj(o_lat, wuv)
    y, bonus, g, wkv_s = rwkv_mix(rw_s, state_shift[0], state_wkv[0].astype(F32), dec_batch, dec_seq)
    h2 = _merge(h1, y, bonus, g, oa, gates, *merge_w)
    h3 = _ffn(h2, *ffn2)
    y_sample = _ple(h3, p_sample[0].reshape(ts_tok, -1), *ple_w)

    return (y_prompt.reshape(batch, seq, d_model), y_sample.reshape(dec_batch, dec_seq, d_model),
            ckv_p.reshape(1, batch, seq, kv_rank), kpe_p.reshape(1, batch, seq, ROPE_DIM),
            wkv_p[None], rw_p.reshape(batch, seq, shift_dim)[:, -1][None],
            ckv_s.reshape(1, dec_batch, dec_seq, kv_rank), kpe_s.reshape(1, dec_batch, dec_seq, ROPE_DIM),
            wkv_s[None], rw_s.reshape(dec_batch, dec_seq, shift_dim)[:, -1][None])
```

```python
import functools
import math

import jax
import jax.numpy as jnp
from jax import lax
from jax.experimental import pallas as pl
from jax.experimental.pallas import tpu as pltpu

F32 = jnp.float32
BF16 = jnp.bfloat16

MLA_HEADS = 8
NOPE_DIM = 64
ROPE_DIM = 32
V_DIM = 64
ROPE_THETA = 10000.0
SOFTMAX_SCALE = (NOPE_DIM + ROPE_DIM) ** -0.5
Q_SCALE = SOFTMAX_SCALE * math.log2(math.e)
RWKV_HEADS = 8
RWKV_HEAD = 64
RWKV_DIM = RWKV_HEADS * RWKV_HEAD
W_LORA = 64
A_LORA = 64
G_LORA = 128
GN_EPS = 64e-5
NORM_EPS = 1e-6

LANES = 128
VMEM_LIMIT_BYTES = 56 * 1024 * 1024

ROPE_PAD = LANES
SCAN_BATCH = 8
HALF_HEAD = RWKV_HEAD // 2
SCAN_ROW_SPLIT = 2
RELAYOUT_UNROLL = 4
SCRATCH_ROW_PAD = 8
MAX_PAGES_PER_STREAM = 16
PAGED_STREAMS = 2


def _cparams(*sem):
    return pltpu.CompilerParams(dimension_semantics=sem, vmem_limit_bytes=VMEM_LIMIT_BYTES)


def _dot(a, b):
    return jnp.dot(a, b, preferred_element_type=F32)


def _dot_nt(a, b):
    return lax.dot_general(a, b, (((1,), (1,)), ((), ())), preferred_element_type=F32)


def _rms(x, g):
    return x * lax.rsqrt(jnp.mean(x * x, axis=-1, keepdims=True) + NORM_EPS) * g


def _sigmoid(x):
    return 1.0 / (1.0 + jnp.exp(-x))


def _rope(x, cos, sin_signed):
    lane = lax.broadcasted_iota(jnp.int32, x.shape, 1)
    swapped = jnp.where((lane % ROPE_DIM) < ROPE_DIM // 2,
                        pltpu.roll(x, LANES - ROPE_DIM // 2, 1), pltpu.roll(x, ROPE_DIM // 2, 1))
    return x * cos + swapped * sin_signed


def _segsum(x, ones_bf16):
    hi = x.astype(BF16)
    lo = (x - hi.astype(F32)).astype(BF16)
    return _dot(hi, ones_bf16) + _dot(lo, ones_bf16)


def _row_tile(n, cap):
    t = cap
    while n % t:
        t //= 2
    assert t >= 8, (n, cap)
    return t


def _const_spec(shape):
    nd = len(shape)
    return pl.BlockSpec(shape, lambda *_: (0,) * nd)


def _rows_spec(tm, width):
    return pl.BlockSpec((tm, width), lambda i: (i, 0))


def _ffn_kernel(x_ref, pre_ref, wg_ref, wu_ref, wd_ref, post_ref, o_ref, *, n_chunks):
    x = x_ref[...]
    xn = _rms(x, pre_ref[...]).astype(BF16)
    fc = wg_ref.shape[1] // n_chunks
    acc = None
    for c in range(n_chunks):
        g = _dot(xn, wg_ref[:, c * fc:(c + 1) * fc])
        u = _dot(xn, wu_ref[:, c * fc:(c + 1) * fc])
        a = (g * _sigmoid(g) * u).astype(BF16)
        d = _dot(a, wd_ref[c * fc:(c + 1) * fc, :])
        acc = d if acc is None else acc + d
    o_ref[...] = x + 0.5 * _rms(acc, post_ref[...])


def _ffn(x, pre_g, wg, wu, wd, post_g):
    t, d = x.shape
    f = wg.shape[1]
    tm = _row_tile(t, 512)
    n_chunks = 2 if f % (2 * LANES) == 0 else 1
    return pl.pallas_call(
        functools.partial(_ffn_kernel, n_chunks=n_chunks),
        out_shape=jax.ShapeDtypeStruct((t, d), F32),
        grid=(t // tm,),
        in_specs=[_rows_spec(tm, d), _const_spec((1, d)), _const_spec((d, f)), _const_spec((d, f)),
                  _const_spec((f, d)), _const_spec((1, d))],
        out_specs=_rows_spec(tm, d),
        compiler_params=_cparams("parallel"),
        name="ffn_half_step",
    )(x, pre_g, wg, wu, wd, post_g)


def _inproj_kernel(h_ref, g_ref, wq_ref, wkv_ref, wkpe_ref, wrw_ref, wgate_ref, qng_ref, wuq_ref, kvg_ref,
                   wukt_ref, cos_ref, sin_ref, qcat_ref, ckv_ref, kpe_ref, rw_ref, gates_ref, *kt_refs, tk):
    u = _rms(h_ref[...], g_ref[...]).astype(BF16)
    rw_ref[...] = _dot(u, wrw_ref[...])
    gates_ref[...] = _dot(u, wgate_ref[...])
    ckv = _rms(_dot(u, wkv_ref[...]), kvg_ref[...])
    ckv_ref[...] = ckv
    cos = cos_ref[...]
    sin = sin_ref[...]
    kpe = _rope(_dot(u, wkpe_ref[...]), cos, sin)
    kpe_ref[...] = kpe[:, :ROPE_DIM]
    kv_rank = ckv.shape[1]
    if kt_refs:
        kcat_ref, ckvt_ref = kt_refs
        kcat_ref[:, :kv_rank] = ckv.astype(BF16)
        kcat_ref[:, kv_rank:] = kpe.astype(BF16)
        for j in range(ckv.shape[0] // tk):
            ckvt_ref[j] = ckv[j * tk:(j + 1) * tk, :].T.astype(BF16)
    cq = _rms(_dot(u, wq_ref[...]), qng_ref[...]).astype(BF16)
    q = _dot(cq, wuq_ref[...])
    nope_w = MLA_HEADS * NOPE_DIM
    qw = kv_rank + ROPE_PAD
    for h in range(MLA_HEADS):
        nope = q[:, h * NOPE_DIM:(h + 1) * NOPE_DIM].astype(BF16)
        qcat_ref[:, h * qw:h * qw + kv_rank] = (_dot(nope, wukt_ref[h]) * Q_SCALE).astype(BF16)
        pe = q[:, nope_w + h * ROPE_PAD: nope_w + (h + 1) * ROPE_PAD]
        qcat_ref[:, h * qw + kv_rank:(h + 1) * qw] = (_rope(pe, cos, sin) * Q_SCALE).astype(BF16)


def _inproj(h, g, wq, wkv, wkpe, wrw, wgate, qng, wuq, kvg, wukt, cos, sin, tk):
    t, d = h.shape
    tm = _row_tile(t, 512)
    kv_rank = wkv.shape[1]
    qw = kv_rank + ROPE_PAD
    n_pos = cos.shape[0] // tm
    out_shape = [jax.ShapeDtypeStruct((t, MLA_HEADS * qw), BF16), jax.ShapeDtypeStruct((t, kv_rank), F32),
                 jax.ShapeDtypeStruct((t, ROPE_DIM), F32), jax.ShapeDtypeStruct((t, wrw.shape[1]), F32),
                 jax.ShapeDtypeStruct((t, wgate.shape[1]), F32)]
    out_specs = [_rows_spec(tm, s.shape[1]) for s in out_shape]
    if tk:
        assert tm % tk == 0
        out_shape += [jax.ShapeDtypeStruct((t, qw), BF16), jax.ShapeDtypeStruct((t // tk, kv_rank, tk), BF16)]
        out_specs += [_rows_spec(tm, qw), pl.BlockSpec((tm // tk, kv_rank, tk), lambda i: (i, 0, 0))]
    consts = (g, wq, wkv, wkpe, wrw, wgate, qng, wuq, kvg, wukt)
    pos_spec = pl.BlockSpec((tm, ROPE_PAD), lambda i: (i % n_pos, 0))
    return pl.pallas_call(
        functools.partial(_inproj_kernel, tk=tk),
        out_shape=out_shape,
        grid=(t // tm,),
        in_specs=[_rows_spec(tm, d)] + [_const_spec(c.shape) for c in consts] + [pos_spec, pos_spec],
        out_specs=out_specs,
        compiler_params=_cparams("parallel"),
        name="mixer_in_proj",
    )(h, *consts, cos, sin)


def _attn_kernel(qcat_ref, kcat_ref, ckvt_ref, wuvt_ref, o_ref, m_scr, l_scr, acc_scr, *, tq):
    i = pl.program_id(1)
    qw = kcat_ref.shape[1]
    m_scr[...] = jnp.full(m_scr.shape, -jnp.inf, F32)
    l_scr[...] = jnp.zeros(l_scr.shape, F32)
    acc_scr[...] = jnp.zeros(acc_scr.shape, F32)

    def block(kb, diagonal):
        off = pl.multiple_of(kb * tq, tq)
        kcat = kcat_ref[pl.ds(off, tq), :]
        vt = ckvt_ref[kb]
        def scores(h):
            return _dot_nt(kcat, qcat_ref[:, h * qw:(h + 1) * qw])

        s_next = scores(0)
        for h in range(MLA_HEADS):
            s, s_next = s_next, (scores(h + 1) if h + 1 < MLA_HEADS else None)
            if diagonal:
                key = lax.broadcasted_iota(jnp.int32, s.shape, 0)
                qry = lax.broadcasted_iota(jnp.int32, s.shape, 1)
                s = jnp.where(key <= qry, s, -jnp.inf)
            m_prev = m_scr[h]
            m_new = jnp.maximum(m_prev, jnp.max(s, axis=0, keepdims=True))
            alpha = jnp.exp2(m_prev - m_new)
            p = jnp.exp2(s - m_new)
            l_scr[h] = alpha * l_scr[h] + jnp.sum(p, axis=0, keepdims=True)
            acc_scr[h] = alpha * acc_scr[h] + _dot(vt, p.astype(BF16))
            m_scr[h] = m_new

    def off_diagonal(kb, carry):
        block(kb, False)
        return carry

    lax.fori_loop(0, i, off_diagonal, 0)
    block(i, True)
    outs = [_dot(wuvt_ref[h], (acc_scr[h] / l_scr[h]).astype(BF16)) for h in range(MLA_HEADS)]
    o_ref[...] = jnp.concatenate(outs, axis=0).T


def _attn_prompt(qcat, kcat, ckvt, wuvt, batch, seq, tq):
    qw = kcat.shape[1]
    kv_rank = ckvt.shape[1]
    nq = seq // tq
    return pl.pallas_call(
        functools.partial(_attn_kernel, tq=tq),
        out_shape=jax.ShapeDtypeStruct((batch * seq, MLA_HEADS * V_DIM), F32),
        grid=(batch, nq),
        in_specs=[pl.BlockSpec((tq, MLA_HEADS * qw), lambda b, i: (b * nq + i, 0)),
                  pl.BlockSpec((seq, qw), lambda b, i: (b, 0)),
                  pl.BlockSpec((nq, kv_rank, tq), lambda b, i: (b, 0, 0)),
                  _const_spec(wuvt.shape)],
        out_specs=pl.BlockSpec((tq, MLA_HEADS * V_DIM), lambda b, i: (b * nq + i, 0)),
        scratch_shapes=[pltpu.VMEM((MLA_HEADS, 1, tq), F32), pltpu.VMEM((MLA_HEADS, 1, tq), F32),
                        pltpu.VMEM((MLA_HEADS, kv_rank, tq), F32)],
        compiler_params=_cparams("parallel", "arbitrary"),
        name="mla_prompt_attention",
    )(qcat, kcat, ckvt, wuvt)


def _paged_kernel(pt_ref, q_ref, knew_ref, kpnew_ref, cache_ckv, cache_kpe_t, o_ref,
                  kc_buf, kp_buf, sem, *, n_batch, chunks_per_batch, pages_per_stream, page, dec_seq):
    total = n_batch * chunks_per_batch
    pages_per_chunk = PAGED_STREAMS * pages_per_stream
    kv_rank = kc_buf.shape[2]

    def page_copies(g, slot, p):
        pg = pt_ref[g * pages_per_chunk + p]
        rows = pl.ds(p * page, page)
        return (pltpu.make_async_copy(cache_ckv.at[0, pg], kc_buf.at[slot, rows, :], sem.at[0, slot]),
                pltpu.make_async_copy(cache_kpe_t.at[0, pg], kp_buf.at[slot, :, rows], sem.at[1, slot]))

    def start_chunk(g, slot):
        for p in range(pages_per_chunk):
            for cp in page_copies(g, slot, p):
                cp.start()

    def wait_chunk(g, slot):
        for p in range(pages_per_chunk):
            for cp in page_copies(g, slot, p):
                cp.wait()

    start_chunk(0, 0)

    def batch_body(b, carry):
        qf = q_ref[b]
        qlf = qf[:, :kv_rank]
        qpf = qf[:, kv_rank:kv_rank + ROPE_DIM]
        qb = qlf.astype(BF16)
        qpb = qpf.astype(BF16)
        rows = qf.shape[0]
        stream_rows = pages_per_stream * page

        def chunk_body(c, states):
            g = b * chunks_per_batch + c
            slot = g % 2

            @pl.when(g + 1 < total)
            def _():
                start_chunk(g + 1, 1 - slot)

            wait_chunk(g, slot)
            keys, scores = [], []
            for st in range(PAGED_STREAMS):
                kc = kc_buf[slot, st * stream_rows:(st + 1) * stream_rows, :].astype(BF16)
                kp_t = kp_buf[slot, :, st * stream_rows:(st + 1) * stream_rows].astype(BF16)
                keys.append(kc)
                scores.append(_dot_nt(qb, kc) + _dot(qpb, kp_t))
            new_states = []
            for (m_prev, l_prev, acc), kc, s in zip(states, keys, scores):
                m_new = jnp.maximum(m_prev, jnp.max(s, axis=-1, keepdims=True))
                alpha = jnp.exp2(m_prev - m_new)
                p = jnp.exp2(s - m_new)
                l_new = alpha * l_prev + jnp.sum(p, axis=-1, keepdims=True)
                new_states.append((m_new, l_new, alpha * acc + _dot(p.astype(BF16), kc)))
            return tuple(new_states)

        init = tuple((jnp.full((rows, 1), -jnp.inf, F32), jnp.zeros((rows, 1), F32), jnp.zeros(qlf.shape, F32))
                     for _ in range(PAGED_STREAMS))
        states = lax.fori_loop(0, chunks_per_batch, chunk_body, init)

        knew = knew_ref[b]
        kpnew = kpnew_ref[b]
        tok = lax.broadcasted_iota(jnp.int32, (rows, 1), 0) % dec_seq
        s_new = []
        for j in range(dec_seq):
            sj = (jnp.sum(qlf * knew[j:j + 1, :], axis=-1, keepdims=True)
                  + jnp.sum(qpf * kpnew[j:j + 1, :], axis=-1, keepdims=True))
            s_new.append(jnp.where(tok >= j, sj, -jnp.inf))
        m_new = states[0][0]
        for m_st, _, _ in states[1:]:
            m_new = jnp.maximum(m_new, m_st)
        for sj in s_new:
            m_new = jnp.maximum(m_new, sj)
        l_new = jnp.zeros((rows, 1), F32)
        acc = jnp.zeros(qlf.shape, F32)
        for m_st, l_st, acc_st in states:
            alpha = jnp.exp2(m_st - m_new)
            l_new = l_new + alpha * l_st
            acc = acc + alpha * acc_st
        for j, sj in enumerate(s_new):
            pj = jnp.exp2(sj - m_new)
            l_new = l_new + pj
            acc = acc + pj * knew[j:j + 1, :]
        o_ref[b] = acc / l_new
        return carry

    lax.fori_loop(0, n_batch, batch_body, 0)


def _attn_paged(page_table, q3, knew3, kpnew3, cache_ckv, cache_kpe_t):
    n_batch, n_pages = page_table.shape
    page = cache_ckv.shape[2]
    kv_rank = cache_ckv.shape[3]
    dec_seq = knew3.shape[1]
    pps = MAX_PAGES_PER_STREAM
    while n_pages % (PAGED_STREAMS * pps):
        pps //= 2
    assert pps >= 1, n_pages
    ppc = PAGED_STREAMS * pps
    kernel = functools.partial(_paged_kernel, n_batch=n_batch, chunks_per_batch=n_pages // ppc,
                               pages_per_stream=pps, page=page, dec_seq=dec_seq)
    vmem = lambda shape: pl.BlockSpec(shape, lambda i, pt: (0,) * len(shape))
    any_spec = pl.BlockSpec(memory_space=pl.ANY)
    out_shape = (n_batch, q3.shape[1], kv_rank)
    return pl.pallas_call(
        kernel,
        out_shape=jax.ShapeDtypeStruct(out_shape, F32),
        grid_spec=pltpu.PrefetchScalarGridSpec(
            num_scalar_prefetch=1,
            grid=(1,),
            in_specs=[vmem(q3.shape), vmem(knew3.shape), vmem(kpnew3.shape), any_spec, any_spec],
            out_specs=vmem(out_shape),
            scratch_shapes=[pltpu.VMEM((2, ppc * page, kv_rank), F32),
                            pltpu.VMEM((2, cache_kpe_t.shape[2], ppc * page), F32),
                            pltpu.SemaphoreType.DMA((2, 2))]),
        compiler_params=_cparams("arbitrary"),
        name="mla_paged_attention",
    )(page_table.reshape(-1), q3, knew3, kpnew3, cache_ckv, cache_kpe_t)


def _uv_kernel(o_ref, wuv_ref, out_ref):
    kv_rank = wuv_ref.shape[1]
    outs = [_dot(o_ref[:, h * kv_rank:(h + 1) * kv_rank].astype(BF16), wuv_ref[h]) for h in range(MLA_HEADS)]
    out_ref[...] = jnp.concatenate(outs, axis=-1)


def _uv_proj(o_lat, wuv):
    t = o_lat.shape[0]
    return pl.pallas_call(
        _uv_kernel,
        out_shape=jax.ShapeDtypeStruct((t, MLA_HEADS * V_DIM), F32),
        grid=(1,),
        in_specs=[_const_spec(o_lat.shape), _const_spec(wuv.shape)],
        out_specs=_const_spec((t, MLA_HEADS * V_DIM)),
        compiler_params=_cparams("arbitrary"),
        name="mla_value_up_proj",
    )(o_lat, wuv)


def _softplus(z):
    return jnp.maximum(z, 0.0) + jnp.log(1.0 + jnp.exp(-jnp.abs(z)))


def _rwkv_prep_kernel(rw_ref, first_ref, *refs, seq, tiles_per_seq):
    x = rw_ref[...]
    row = lax.broadcasted_iota(jnp.int32, x.shape, 0)
    rolled = pltpu.roll(x, 1, 0)
    if tiles_per_seq:
        prev8_ref, refs = refs[0], refs[1:]
        at_start = pl.program_id(0) % tiles_per_seq == 0
        first = jnp.where(at_start, first_ref[...], prev8_ref[7:8, :])
        prev = jnp.where(row == 0, first, rolled)
    else:
        prev = jnp.where(row % seq == 0, first_ref[...], rolled)
    (mu_ref, w0_ref, a0_ref, wa_up_ref, g_up_ref, kk_ref, ka_ref, rk_ref, ones_ref,
     r_o, w_o, k_o, v_o, a_o, b_o, g_o, bonus_o) = refs
    xr = x + (prev - x) * mu_ref[...]
    d = RWKV_DIM
    r = xr[:, 0:d]
    k = xr[:, d:2 * d]
    v = xr[:, 2 * d:3 * d]
    xwa = xr[:, 3 * d:3 * d + W_LORA + A_LORA]
    xg = xr[:, 3 * d + W_LORA + A_LORA:]
    lane = lax.broadcasted_iota(jnp.int32, xwa.shape, 1)
    wa = _dot(jnp.where(lane < W_LORA, jnp.tanh(xwa), xwa).astype(BF16), wa_up_ref[...])
    w = -_softplus(-(w0_ref[...] + wa[:, :d])) - 0.5
    a = _sigmoid(a0_ref[...] + wa[:, d:])
    ones = ones_ref[...]
    kk = k * kk_ref[...]
    kk = kk / jnp.maximum(jnp.sqrt(_segsum(kk * kk, ones)), 1e-12)
    k2 = k * (1.0 + (a - 1.0) * ka_ref[...])
    scan_ops = ((r_o, r), (w_o, jnp.exp(-jnp.exp(w))), (k_o, k2), (v_o, v), (a_o, -kk), (b_o, kk * a))
    for o_ref, val in scan_ops:
        o_ref[...] = val.T if tiles_per_seq else val
    g_o[...] =_dot(_sigmoid(xg).astype(BF16), g_up_ref[...])
    bonus_o[...] = _segsum(r * k2 * rk_ref[...], ones) * v


def _rwkv_prep(rw, shift0, seq, mu, w0, a0, wa_up, g_up, k_k, k_a, r_k, ones):
    t, width = rw.shape
    tm = _row_tile(t, 512)
    consts = (mu, w0, a0, wa_up, g_up, k_k, k_a, r_k, ones)
    if seq % tm == 0:
        tiles_per_seq = seq // tm
        first = shift0.reshape(-1, 1, width)
        lead = [first, rw]
        lead_specs = [pl.BlockSpec((None, 1, width), lambda i: (i // tiles_per_seq, 0, 0)),
                      pl.BlockSpec((8, width), lambda i: (jnp.maximum(i * (tm // 8) - 1, 0), 0))]
    else:
        assert tm % seq == 0
        tiles_per_seq = 0
        lead = [jnp.repeat(shift0, seq, axis=0)]
        lead_specs = [_rows_spec(tm, width)]
    out_shape = [jax.ShapeDtypeStruct((t, RWKV_DIM), F32)] * 8
    out_specs = [_rows_spec(tm, RWKV_DIM)] * 8
    if tiles_per_seq:
        out_shape[:6] = [jax.ShapeDtypeStruct((t // seq, RWKV_DIM, seq), F32)] * 6
        out_specs[:6] = [pl.BlockSpec((None, RWKV_DIM, tm), lambda i: (i // tiles_per_seq, 0, i % tiles_per_seq))] * 6
    return pl.pallas_call(
        functools.partial(_rwkv_prep_kernel, seq=seq, tiles_per_seq=tiles_per_seq),
        out_shape=out_shape,
        grid=(t // tm,),
        in_specs=[_rows_spec(tm, width)] + lead_specs + [_const_spec(c.shape) for c in consts],
        out_specs=out_specs,
        compiler_params=_cparams("parallel"),
        name="rwkv_prep",
    )(rw, *lead, *consts)


def _scan_steps(st_ref, ts, row, value_at, put_y):
    group = RWKV_HEAD // SCAN_ROW_SPLIT
    groups = [slice(i * group, (i + 1) * group) for i in range(SCAN_ROW_SPLIT)]

    def state_dot(name, t, rows):
        acc = st_ref[0, rows, :] * row(name, t, 0)
        for kq in range(1, HALF_HEAD):
            acc = acc + st_ref[kq, rows, :] * row(name, t, kq)
        return acc

    def step(t, sa_halves):
        t_next = jnp.minimum(t + 1, ts - 1)
        out = []
        for rows, sa_half in zip(groups, sa_halves):
            sa = sa_half + pltpu.roll(sa_half, LANES // 2, 1)
            vt = value_at(t, rows)
            y = None
            sa_next = None
            for kq in range(HALF_HEAD):
                s_new = st_ref[kq, rows, :] * row("w", t, kq) + sa * row("b", t, kq) + vt * row("k", t, kq)
                st_ref[kq, rows, :] = s_new
                y_term = s_new * row("r", t, kq)
                a_term = s_new * row("a", t_next, kq)
                y = y_term if y is None else y + y_term
                sa_next = a_term if sa_next is None else sa_next + a_term
            put_y(t, rows, y + pltpu.roll(y, LANES // 2, 1))
            out.append(sa_next)
        return tuple(out)

    lax.fori_loop(0, ts, step, tuple(state_dot("a", 0, rows) for rows in groups))


def _scan_kernel(r_ref, w_ref, k_ref, a_ref, b_ref, v_ref, s0_ref, y_ref, sfin_ref, st_ref, *, ts):
    j = pl.program_id(1)

    @pl.when(j == 0)
    def _():
        st_ref[...] = s0_ref[...]

    refs = dict(r=r_ref, w=w_ref, k=k_ref, a=a_ref, b=b_ref)

    def put_y(t, rows, y):
        y_ref[t, rows, :] = y

    _scan_steps(st_ref, ts, lambda name, t, kq: refs[name][t, pl.ds(kq, 1), :], lambda t, rows: v_ref[t, rows, :], put_y)

    @pl.when(j == pl.num_programs(1) - 1)
    def _():
        sfin_ref[...] = st_ref[...]


def _scan_kernel_cm(r_ref, w_ref, k_ref, a_ref, b_ref, v_ref, s0_ref, y_ref, sfin_ref,
                    st_ref, r_s, w_s, k_s, a_s, b_s, v_s, y_s, *, ts):
    j = pl.program_id(1)

    @pl.when(j == 0)
    def _():
        st_ref[...] = s0_ref[...]

    def head_rows(ref, c0):
        return jnp.concatenate([ref[b, pl.ds(c0, RWKV_HEADS, stride=RWKV_HEAD), :] for b in range(SCAN_BATCH)], axis=0)

    for src, dst in ((r_ref, r_s), (w_ref, w_s), (k_ref, k_s), (a_ref, a_s), (b_ref, b_s)):
        def key_tile(kq, carry, src=src, dst=dst):
            tile = jnp.concatenate([head_rows(src, kq), head_rows(src, HALF_HEAD + kq)], axis=0)
            dst[kq, :ts, :] = tile.T
            return carry

        lax.fori_loop(0, HALF_HEAD, key_tile, 0, unroll=RELAYOUT_UNROLL)

    value_rows = lambda v: pl.ds(v, ts, stride=RWKV_HEAD)

    def value_tile(v, carry):
        half = head_rows(v_ref, v)
        v_s[value_rows(v), :] = jnp.concatenate([half, half], axis=0).T
        return carry

    lax.fori_loop(0, RWKV_HEAD, value_tile, 0, unroll=RELAYOUT_UNROLL)

    scr = dict(r=r_s, w=w_s, k=k_s, a=a_s, b=b_s)
    step_rows = lambda t, rows: pl.ds(pl.multiple_of(t * RWKV_HEAD, RWKV_HEAD) + rows.start, rows.stop - rows.start)

    def put_y(t, rows, y):
        y_s[step_rows(t, rows), :] = y

    _scan_steps(st_ref, ts, lambda name, t, kq: scr[name][kq, pl.ds(t, 1), :],
                lambda t, rows: v_s[step_rows(t, rows), :], put_y)

    def y_tile(v, carry):
        tile = y_s[value_rows(v), :].T
        for b in range(SCAN_BATCH):
            y_ref[b, pl.ds(v, RWKV_HEADS, stride=RWKV_HEAD), :] = tile[b * RWKV_HEADS:(b + 1) * RWKV_HEADS, :]
        return carry

    lax.fori_loop(0, RWKV_HEAD, y_tile, 0, unroll=RELAYOUT_UNROLL)

    @pl.when(j == pl.num_programs(1) - 1)
    def _():
        sfin_ref[...] = st_ref[...]


def _wkv_scan_cm(r, w, k, a, b, v, s0):
    batch, _, seq = r.shape
    ts = LANES
    assert seq % ts == 0 and batch % SCAN_BATCH == 0
    cm_spec = pl.BlockSpec((SCAN_BATCH, RWKV_DIM, ts), lambda g, j: (g, 0, j))
    state_spec = pl.BlockSpec((None, HALF_HEAD, RWKV_HEAD, LANES), lambda g, j: (g, 0, 0, 0))
    return pl.pallas_call(
        functools.partial(_scan_kernel_cm, ts=ts),
        out_shape=[jax.ShapeDtypeStruct(r.shape, F32), jax.ShapeDtypeStruct(s0.shape, F32)],
        grid=(batch // SCAN_BATCH, seq // ts),
        in_specs=[cm_spec] * 6 + [state_spec],
        out_specs=[cm_spec, state_spec],
        scratch_shapes=[pltpu.VMEM((HALF_HEAD, RWKV_HEAD, LANES), F32)]
                       + [pltpu.VMEM((HALF_HEAD, ts + SCRATCH_ROW_PAD, LANES), F32)] * 5
                       + [pltpu.VMEM((RWKV_HEAD * ts, LANES), F32)] * 2,
        compiler_params=_cparams("parallel", "arbitrary"),
        name="rwkv7_state_scan",
    )(r, w, k, a, b, v, s0)


def _wkv_scan(r, w, k, a, b, v, s0):
    ng, seq = r.shape[:2]
    ts = _row_tile(seq, 128) if seq % 8 == 0 else seq
    half_spec = pl.BlockSpec((None, ts, HALF_HEAD, LANES), lambda g, j: (g, j, 0, 0))
    full_spec = pl.BlockSpec((None, ts, RWKV_HEAD, LANES), lambda g, j: (g, j, 0, 0))
    state_spec = pl.BlockSpec((None, HALF_HEAD, RWKV_HEAD, LANES), lambda g, j: (g, 0, 0, 0))
    return pl.pallas_call(
        functools.partial(_scan_kernel, ts=ts),
        out_shape=[jax.ShapeDtypeStruct(v.shape, F32), jax.ShapeDtypeStruct(s0.shape, F32)],
        grid=(ng, seq // ts),
        in_specs=[half_spec] * 5 + [full_spec, state_spec],
        out_specs=[full_spec, state_spec],
        scratch_shapes=[pltpu.VMEM((HALF_HEAD, RWKV_HEAD, LANES), F32)],
        compiler_params=_cparams("parallel", "arbitrary"),
        name="rwkv7_state_scan",
    )(r, w, k, a, b, v, s0)


def _to_scan_layout(x, batch, seq, dup):
    ng = batch // SCAN_BATCH
    if dup:
        x = x.reshape(ng, SCAN_BATCH, seq, RWKV_HEADS, RWKV_HEAD).transpose(0, 2, 4, 1, 3)
        x = x.reshape(ng, seq, RWKV_HEAD, SCAN_BATCH * RWKV_HEADS)
        return jnp.concatenate([x, x], axis=-1)
    x = x.reshape(ng, SCAN_BATCH, seq, RWKV_HEADS, 2, HALF_HEAD).transpose(0, 2, 5, 4, 1, 3)
    return x.reshape(ng, seq, HALF_HEAD, LANES)


def _from_scan_layout(y, batch, seq):
    ng = batch // SCAN_BATCH
    y = y[..., :LANES // 2].reshape(ng, seq, RWKV_HEAD, SCAN_BATCH, RWKV_HEADS).transpose(0, 3, 1, 4, 2)
    return y.reshape(batch * seq, RWKV_DIM)


def _state_to_scan_layout(s, batch):
    ng = batch // SCAN_BATCH
    s = s.reshape(ng, SCAN_BATCH, RWKV_HEADS, RWKV_HEAD, 2, HALF_HEAD).transpose(0, 5, 3, 4, 1, 2)
    return s.reshape(ng, HALF_HEAD, RWKV_HEAD, LANES)


def _state_from_scan_layout(s, batch):
    ng = batch // SCAN_BATCH
    s = s.reshape(ng, HALF_HEAD, RWKV_HEAD, 2, SCAN_BATCH, RWKV_HEADS).transpose(0, 4, 5, 2, 3, 1)
    return s.reshape(batch, RWKV_HEADS, RWKV_HEAD, RWKV_HEAD)


def _merge_kernel(h_ref, y_ref, bonus_ref, g_ref, oa_ref, gates_ref, lnw_ref, lnb_ref, ones_ref, wob_ref, woa_ref,
                  wout_ref, post_ref, o_ref, *, y_channel_major):
    ones = ones_ref[...]
    y = y_ref[...].T if y_channel_major else y_ref[...]
    inv_n = 1.0 / RWKV_HEAD
    mu = _segsum(y, ones) * inv_n
    yc = y - mu
    var = _segsum(yc * yc, ones) * inv_n
    yn = yc * lax.rsqrt(var + GN_EPS) * lnw_ref[...] + lnb_ref[...] + bonus_ref[...]
    o_b = _dot((yn * g_ref[...]).astype(BF16), wob_ref[...])
    o_a = _dot(oa_ref[...].astype(BF16), woa_ref[...])
    d = o_ref.shape[1]
    gates = gates_ref[...]
    mix = _sigmoid(gates[:, :d]) * o_a + _sigmoid(gates[:, d:]) * o_b
    o_ref[...] = h_ref[...] + _rms(_dot(mix.astype(BF16), wout_ref[...]), post_ref[...])


def _merge(h, y, bonus, g, oa, gates, lnw, lnb, ones, wob, woa, wout, post_g):
    t, d = h.shape
    tm = _row_tile(t, 512)
    consts = (lnw, lnb, ones, wob, woa, wout, post_g)
    y_spec = _rows_spec(tm, RWKV_DIM)
    if y.ndim == 3:
        tiles_per_seq = y.shape[2] // tm
        y_spec = pl.BlockSpec((None, RWKV_DIM, tm), lambda i: (i // tiles_per_seq, 0, i % tiles_per_seq))
    return pl.pallas_call(
        functools.partial(_merge_kernel, y_channel_major=y.ndim == 3),
        out_shape=jax.ShapeDtypeStruct((t, d), F32),
        grid=(t // tm,),
        in_specs=[_rows_spec(tm, d), y_spec] + [_rows_spec(tm, RWKV_DIM)] * 3 + [_rows_spec(tm, 2 * d)]
                 + [_const_spec(c.shape) for c in consts],
        out_specs=_rows_spec(tm, d),
        compiler_params=_cparams("parallel"),
        name="gated_merge_out_proj",
    )(h, y, bonus, g, oa, gates, *consts)


def _ple_kernel(h_ref, p_ref, pre_ref, wgate_ref, wproj_ref, post_ref, o_ref):
    h = h_ref[...]
    gp = _sigmoid(_dot(_rms(h, pre_ref[...]).astype(BF16), wgate_ref[...]))
    emb = _dot(p_ref[...].astype(BF16), wproj_ref[...])
    o_ref[...] = h + _rms(gp * emb, post_ref[...])


def _ple(h, p, pre_g, wgate, wproj, post_g):
    t, d = h.shape
    tm = _row_tile(t, 512)
    consts = (pre_g, wgate, wproj, post_g)
    return pl.pallas_call(
        _ple_kernel,
        out_shape=jax.ShapeDtypeStruct((t, d), F32),
        grid=(t // tm,),
        in_specs=[_rows_spec(tm, d), _rows_spec(tm, p.shape[1])] + [_const_spec(c.shape) for c in consts],
        out_specs=_rows_spec(tm, d),
        compiler_params=_cparams("parallel"),
        name="per_layer_embedding",
    )(h, p, *consts)


def _rope_tables(pos):
    inv = ROPE_THETA ** (-jnp.arange(0, ROPE_DIM, 2, dtype=F32) / ROPE_DIM)
    ang = pos[:, None] * inv[None, :]
    cos, sin = jnp.cos(ang), jnp.sin(ang)
    reps = ROPE_PAD // ROPE_DIM
    return jnp.tile(jnp.concatenate([cos, cos], axis=-1), (1, reps)), jnp.tile(jnp.concatenate([-sin, sin], axis=-1), (1, reps))


def _row(v):
    return v.reshape(1, -1).astype(F32)


def kernel(x_prompt, x_sample, p_prompt, p_sample, cache_ckv, cache_kpe, state_wkv, state_shift, page_table, ffn1_pre_g, ffn1_wg, ffn1_wu, ffn1_wd, ffn1_post_g, mix_pre_g, w_in, q_norm_g, w_uq, kv_norm_g, w_uk, w_uv, w_oa, rwkv_mu, rwkv_w0, rwkv_w_up, rwkv_a0, rwkv_a_up, rwkv_g_up, rwkv_k_k, rwkv_k_a, rwkv_r_k, rwkv_ln_w, rwkv_ln_b, w_ob, w_out, mix_post_g, ffn2_pre_g, ffn2_wg, ffn2_wu, ffn2_wd, ffn2_post_g, ple_pre_g, w_ple_gate, w_ple_proj, ple_post_g):
    depth = w_in.shape[0]
    assert depth == 1, "single-layer step"
    batch, seq, d_model = x_prompt.shape
    dec_batch, dec_seq, _ = x_sample.shape
    n_pages = page_table.shape[1]
    page = cache_ckv.shape[2]
    kv_rank = cache_ckv.shape[3]
    q_rank = w_uq.shape[1]
    shift_dim = state_shift.shape[2]
    tp, ts_tok = batch * seq, dec_batch * dec_seq
    assert batch % SCAN_BATCH == 0 and dec_batch % SCAN_BATCH == 0
    assert shift_dim == 3 * RWKV_DIM + W_LORA + A_LORA + G_LORA
    bf = lambda w: w.astype(BF16)

    w_in0 = w_in[0]
    c0, c1, c2, c3 = q_rank, q_rank + kv_rank, q_rank + kv_rank + ROPE_DIM, q_rank + kv_rank + ROPE_DIM + shift_dim
    wq, wkv, wrw, wgate = bf(w_in0[:, :c0]), bf(w_in0[:, c0:c1]), bf(w_in0[:, c2:c3]), bf(w_in0[:, c3:])
    wkpe = bf(jnp.pad(w_in0[:, c1:c2], ((0, 0), (0, ROPE_PAD - ROPE_DIM))))
    wuq3 = w_uq[0].reshape(q_rank, MLA_HEADS, NOPE_DIM + ROPE_DIM)
    wuq_nope = wuq3[:, :, :NOPE_DIM].reshape(q_rank, MLA_HEADS * NOPE_DIM)
    wuq_pe = jnp.pad(wuq3[:, :, NOPE_DIM:], ((0, 0), (0, 0), (0, ROPE_PAD - ROPE_DIM))).reshape(q_rank, MLA_HEADS * ROPE_PAD)
    wuq = bf(jnp.concatenate([wuq_nope, wuq_pe], axis=1))
    wukt = bf(w_uk[0].transpose(1, 2, 0))
    wuv = bf(w_uv[0].transpose(1, 0, 2))
    wuvt = bf(w_uv[0].transpose(1, 2, 0))
    wa_up = jnp.zeros((W_LORA + A_LORA, 2 * RWKV_DIM), F32)
    wa_up = bf(wa_up.at[:W_LORA, :RWKV_DIM].set(rwkv_w_up[0]).at[W_LORA:, RWKV_DIM:].set(rwkv_a_up[0]))
    seg = jnp.arange(RWKV_DIM) // RWKV_HEAD
    ones = (seg[:, None] == seg[None, :]).astype(BF16)
    ffn1 = (_row(ffn1_pre_g), bf(ffn1_wg[0]), bf(ffn1_wu[0]), bf(ffn1_wd[0]), _row(ffn1_post_g))
    ffn2 = (_row(ffn2_pre_g), bf(ffn2_wg[0]), bf(ffn2_wu[0]), bf(ffn2_wd[0]), _row(ffn2_post_g))
    inproj_w = (_row(mix_pre_g), wq, wkv, wkpe, wrw, wgate, _row(q_norm_g), wuq, _row(kv_norm_g), wukt)
    prep_w = (_row(rwkv_mu), _row(rwkv_w0), _row(rwkv_a0), wa_up, bf(rwkv_g_up[0]), _row(rwkv_k_k), _row(rwkv_k_a),
              _row(rwkv_r_k), ones)
    merge_w = (_row(rwkv_ln_w), _row(rwkv_ln_b), ones, bf(w_ob[0]), bf(w_oa[0]), bf(w_out[0]), _row(mix_post_g))
    ple_w = (_row(ple_pre_g), bf(w_ple_gate[0]), bf(w_ple_proj[0]), _row(ple_post_g))

    def rwkv_mix(rw, shift0, s0, nb, ns):
        r, w, k, v, a, b, g, bonus = _rwkv_prep(rw, shift0, ns, *prep_w)
        if r.ndim == 3:
            y, s_fin = _wkv_scan_cm(r, w, k, a, b, v, _state_to_scan_layout(s0, nb))
            return y, bonus, g, _state_from_scan_layout(s_fin, nb)
        halves = [_to_scan_layout(t_, nb, ns, False) for t_ in (r, w, k, a, b)]
        y, s_fin = _wkv_scan(*halves, _to_scan_layout(v, nb, ns, True), _state_to_scan_layout(s0, nb))
        return _from_scan_layout(y, nb, ns), bonus, g, _state_from_scan_layout(s_fin, nb)

    tq = _row_tile(seq, 256)
    cos_p, sin_p = _rope_tables(jnp.arange(seq, dtype=F32))
    h1 = _ffn(x_prompt.reshape(tp, d_model), *ffn1)
    qcat, ckv_p, kpe_p, rw_p, gates, kcat, ckvt = _inproj(h1, *inproj_w, cos_p, sin_p, tq)
    oa = _attn_prompt(qcat, kcat, ckvt, wuvt, batch, seq, tq)
    y, bonus, g, wkv_p = rwkv_mix(rw_p, jnp.zeros((batch, shift_dim), F32),
                                  jnp.zeros((batch, RWKV_HEADS, RWKV_HEAD, RWKV_HEAD), F32), batch, seq)
    h2 = _merge(h1, y, bonus, g, oa, gates, *merge_w)
    h3 = _ffn(h2, *ffn2)
    y_prompt = _ple(h3, p_prompt[0].reshape(tp, -1), *ple_w)

    pos_s = jnp.tile(n_pages * page + jnp.arange(dec_seq, dtype=F32), dec_batch)
    cos_s, sin_s = _rope_tables(pos_s)
    h1 = _ffn(x_sample.reshape(ts_tok, d_model), *ffn1)
    qcat, ckv_s, kpe_s, rw_s, gates = _inproj(h1, *inproj_w, cos_s, sin_s, 0)
    qw = kv_rank + ROPE_PAD
    q3 = qcat.reshape(dec_batch, dec_seq, MLA_HEADS, qw).transpose(0, 2, 1, 3).reshape(
        dec_batch, MLA_HEADS * dec_seq, qw).astype(F32)
    o3 = _attn_paged(page_table, q3, ckv_s.reshape(dec_batch, dec_seq, kv_rank),
                     kpe_s.reshape(dec_batch, dec_seq, ROPE_DIM), cache_ckv, jnp.swapaxes(cache_kpe, 2, 3))
    o_lat = o3.reshape(dec_batch, MLA_HEADS, dec_seq, kv_rank).transpose(0, 2, 1, 3).reshape(ts_tok, MLA_HEADS * kv_rank)
    oa = _uv_proj(o_lat, wuv)
    y, bonus, g, wkv_s = rwkv_mix(rw_s, state_shift[0], state_wkv[0].astype(F32), dec_batch, dec_seq)
    h2 = _merge(h1, y, bonus, g, oa, gates, *merge_w)
    h3 = _ffn(h2, *ffn2)
    y_sample = _ple(h3, p_sample[0].reshape(ts_tok, -1), *ple_w)

    return (y_prompt.reshape(batch, seq, d_model), y_sample.reshape(dec_batch, dec_seq, d_model),
            ckv_p.reshape(1, batch, seq, kv_rank), kpe_p.reshape(1, batch, seq, ROPE_DIM),
            wkv_p[None], rw_p.reshape(batch, seq, shift_dim)[:, -1][None],
            ckv_s.reshape(1, dec_batch, dec_seq, kv_rank), kpe_s.reshape(1, dec_batch, dec_seq, ROPE_DIM),
            wkv_s[None], rw_s.reshape(dec_batch, dec_seq, shift_dim)[:, -1][None])
```

```python
import functools
import math

import jax
import jax.numpy as jnp
from jax import lax
from jax.experimental import pallas as pl
from jax.experimental.pallas import tpu as pltpu

F32 = jnp.float32
BF16 = jnp.bfloat16

MLA_HEADS = 8
NOPE_DIM = 64
ROPE_DIM = 32
V_DIM = 64
ROPE_THETA = 10000.0
SOFTMAX_SCALE = (NOPE_DIM + ROPE_DIM) ** -0.5
Q_SCALE = SOFTMAX_SCALE * math.log2(math.e)
RWKV_HEADS = 8
RWKV_HEAD = 64
RWKV_DIM = RWKV_HEADS * RWKV_HEAD
W_LORA = 64
A_LORA = 64
G_LORA = 128
GN_EPS = 64e-5
NORM_EPS = 1e-6

LANES = 128
VMEM_LIMIT_BYTES = 56 * 1024 * 1024

ROPE_PAD = LANES
SCAN_BATCH = 8
HALF_HEAD = RWKV_HEAD // 2
SCAN_ROW_SPLIT = 2
RELAYOUT_UNROLL = 4
SCRATCH_ROW_PAD = 8
MAX_PAGES_PER_STREAM = 16
PAGED_STREAMS = 2
PAGED_SLOTS = 3
SCORE_LOOKAHEAD = 8


def _cparams(*sem):
    return pltpu.CompilerParams(dimension_semantics=sem, vmem_limit_bytes=VMEM_LIMIT_BYTES)


def _dot(a, b):
    return jnp.dot(a, b, preferred_element_type=F32)


def _dot_nt(a, b):
    return lax.dot_general(a, b, (((1,), (1,)), ((), ())), preferred_element_type=F32)


def _rms(x, g):
    return x * lax.rsqrt(jnp.mean(x * x, axis=-1, keepdims=True) + NORM_EPS) * g


def _sigmoid(x):
    return 1.0 / (1.0 + jnp.exp(-x))


def _rope(x, cos, sin_signed):
    lane = lax.broadcasted_iota(jnp.int32, x.shape, 1)
    swapped = jnp.where((lane % ROPE_DIM) < ROPE_DIM // 2,
                        pltpu.roll(x, LANES - ROPE_DIM // 2, 1), pltpu.roll(x, ROPE_DIM // 2, 1))
    return x * cos + swapped * sin_signed


def _segsum(x, ones_bf16):
    hi = x.astype(BF16)
    lo = (x - hi.astype(F32)).astype(BF16)
    return _dot(hi, ones_bf16) + _dot(lo, ones_bf16)


def _row_tile(n, cap):
    t = cap
    while n % t:
        t //= 2
    assert t >= 8, (n, cap)
    return t


def _const_spec(shape):
    nd = len(shape)
    return pl.BlockSpec(shape, lambda *_: (0,) * nd)


def _rows_spec(tm, width):
    return pl.BlockSpec((tm, width), lambda i: (i, 0))


def _ffn_kernel(x_ref, pre_ref, wg_ref, wu_ref, wd_ref, post_ref, o_ref, *, n_chunks):
    x = x_ref[...]
    xn = _rms(x, pre_ref[...]).astype(BF16)
    fc = wg_ref.shape[1] // n_chunks
    acc = None
    for c in range(n_chunks):
        g = _dot(xn, wg_ref[:, c * fc:(c + 1) * fc])
        u = _dot(xn, wu_ref[:, c * fc:(c + 1) * fc])
        a = (g * _sigmoid(g) * u).astype(BF16)
        d = _dot(a, wd_ref[c * fc:(c + 1) * fc, :])
        acc = d if acc is None else acc + d
    o_ref[...] = x + 0.5 * _rms(acc, post_ref[...])


def _ffn(x, pre_g, wg, wu, wd, post_g):
    t, d = x.shape
    f = wg.shape[1]
    tm = _row_tile(t, 512)
    n_chunks = 2 if f % (2 * LANES) == 0 else 1
    return pl.pallas_call(
        functools.partial(_ffn_kernel, n_chunks=n_chunks),
        out_shape=jax.ShapeDtypeStruct((t, d), F32),
        grid=(t // tm,),
        in_specs=[_rows_spec(tm, d), _const_spec((1, d)), _const_spec((d, f)), _const_spec((d, f)),
                  _const_spec((f, d)), _const_spec((1, d))],
        out_specs=_rows_spec(tm, d),
        compiler_params=_cparams("parallel"),
        name="ffn_half_step",
    )(x, pre_g, wg, wu, wd, post_g)


def _inproj_kernel(h_ref, g_ref, wq_ref, wkv_ref, wkpe_ref, wrw_ref, wgate_ref, qng_ref, wuq_ref, kvg_ref,
                   wukt_ref, cos_ref, sin_ref, qcat_ref, ckv_ref, kpe_ref, rw_ref, gates_ref, *kt_refs, tk):
    u = _rms(h_ref[...], g_ref[...]).astype(BF16)
    rw_ref[...] = _dot(u, wrw_ref[...])
    gates_ref[...] = _dot(u, wgate_ref[...])
    ckv = _rms(_dot(u, wkv_ref[...]), kvg_ref[...])
    ckv_ref[...] = ckv
    cos = cos_ref[...]
    sin = sin_ref[...]
    kpe = _rope(_dot(u, wkpe_ref[...]), cos, sin)
    kpe_ref[...] = kpe[:, :ROPE_DIM]
    kv_rank = ckv.shape[1]
    if kt_refs:
        kcat_ref, ckvt_ref = kt_refs
        kcat_ref[:, :kv_rank] = ckv.astype(BF16)
        kcat_ref[:, kv_rank:] = kpe.astype(BF16)
        for j in range(ckv.shape[0] // tk):
            ckvt_ref[j] = ckv[j * tk:(j + 1) * tk, :].T.astype(BF16)
    cq = _rms(_dot(u, wq_ref[...]), qng_ref[...]).astype(BF16)
    q = _dot(cq, wuq_ref[...])
    nope_w = MLA_HEADS * NOPE_DIM
    qw = kv_rank + ROPE_PAD
    for h in range(MLA_HEADS):
        nope = q[:, h * NOPE_DIM:(h + 1) * NOPE_DIM].astype(BF16)
        qcat_ref[:, h * qw:h * qw + kv_rank] = (_dot(nope, wukt_ref[h]) * Q_SCALE).astype(BF16)
        pe = q[:, nope_w + h * ROPE_PAD: nope_w + (h + 1) * ROPE_PAD]
        qcat_ref[:, h * qw + kv_rank:(h + 1) * qw] = (_rope(pe, cos, sin) * Q_SCALE).astype(BF16)


def _inproj(h, g, wq, wkv, wkpe, wrw, wgate, qng, wuq, kvg, wukt, cos, sin, tk):
    t, d = h.shape
    tm = _row_tile(t, 512)
    kv_rank = wkv.shape[1]
    qw = kv_rank + ROPE_PAD
    n_pos = cos.shape[0] // tm
    out_shape = [jax.ShapeDtypeStruct((t, MLA_HEADS * qw), BF16), jax.ShapeDtypeStruct((t, kv_rank), F32),
                 jax.ShapeDtypeStruct((t, ROPE_DIM), F32), jax.ShapeDtypeStruct((t, wrw.shape[1]), F32),
                 jax.ShapeDtypeStruct((t, wgate.shape[1]), F32)]
    out_specs = [_rows_spec(tm, s.shape[1]) for s in out_shape]
    if tk:
        assert tm % tk == 0
        out_shape += [jax.ShapeDtypeStruct((t, qw), BF16), jax.ShapeDtypeStruct((t // tk, kv_rank, tk), BF16)]
        out_specs += [_rows_spec(tm, qw), pl.BlockSpec((tm // tk, kv_rank, tk), lambda i: (i, 0, 0))]
    consts = (g, wq, wkv, wkpe, wrw, wgate, qng, wuq, kvg, wukt)
    pos_spec = pl.BlockSpec((tm, ROPE_PAD), lambda i: (i % n_pos, 0))
    return pl.pallas_call(
        functools.partial(_inproj_kernel, tk=tk),
        out_shape=out_shape,
        grid=(t // tm,),
        in_specs=[_rows_spec(tm, d)] + [_const_spec(c.shape) for c in consts] + [pos_spec, pos_spec],
        out_specs=out_specs,
        compiler_params=_cparams("parallel"),
        name="mixer_in_proj",
    )(h, *consts, cos, sin)


def _attn_kernel(qcat_ref, kcat_ref, ckvt_ref, wuvt_ref, o_ref, m_scr, l_scr, acc_scr, *, tq):
    i = pl.program_id(1)
    qw = kcat_ref.shape[1]
    m_scr[...] = jnp.full(m_scr.shape, -jnp.inf, F32)
    l_scr[...] = jnp.zeros(l_scr.shape, F32)
    acc_scr[...] = jnp.zeros(acc_scr.shape, F32)

    def block(kb, diagonal):
        off = pl.multiple_of(kb * tq, tq)
        kcat = kcat_ref[pl.ds(off, tq), :]
        vt = ckvt_ref[kb]
        def scores(h):
            return _dot_nt(kcat, qcat_ref[:, h * qw:(h + 1) * qw])

        pending = [scores(h) for h in range(min(SCORE_LOOKAHEAD, MLA_HEADS))]
        for h in range(MLA_HEADS):
            if h + SCORE_LOOKAHEAD < MLA_HEADS:
                pending.append(scores(h + SCORE_LOOKAHEAD))
            s = pending.pop(0)
            if diagonal:
                key = lax.broadcasted_iota(jnp.int32, s.shape, 0)
                qry = lax.broadcasted_iota(jnp.int32, s.shape, 1)
                s = jnp.where(key <= qry, s, -jnp.inf)
            m_prev = m_scr[h]
            m_new = jnp.maximum(m_prev, jnp.max(s, axis=0, keepdims=True))
            alpha = jnp.exp2(m_prev - m_new)
            p = jnp.exp2(s - m_new)
            l_scr[h] = alpha * l_scr[h] + jnp.sum(p, axis=0, keepdims=True)
            acc_scr[h] = alpha * acc_scr[h] + _dot(vt, p.astype(BF16))
            m_scr[h] = m_new

    def off_diagonal(kb, carry):
        block(kb, False)
        return carry

    lax.fori_loop(0, i, off_diagonal, 0)
    block(i, True)
    outs = [_dot(wuvt_ref[h], (acc_scr[h] / l_scr[h]).astype(BF16)) for h in range(MLA_HEADS)]
    o_ref[...] = jnp.concatenate(outs, axis=0).T


def _attn_prompt(qcat, kcat, ckvt, wuvt, batch, seq, tq):
    qw = kcat.shape[1]
    kv_rank = ckvt.shape[1]
    nq = seq // tq
    return pl.pallas_call(
        functools.partial(_attn_kernel, tq=tq),
        out_shape=jax.ShapeDtypeStruct((batch * seq, MLA_HEADS * V_DIM), F32),
        grid=(batch, nq),
        in_specs=[pl.BlockSpec((tq, MLA_HEADS * qw), lambda b, i: (b * nq + i, 0)),
                  pl.BlockSpec((seq, qw), lambda b, i: (b, 0)),
                  pl.BlockSpec((nq, kv_rank, tq), lambda b, i: (b, 0, 0)),
                  _const_spec(wuvt.shape)],
        out_specs=pl.BlockSpec((tq, MLA_HEADS * V_DIM), lambda b, i: (b * nq + i, 0)),
        scratch_shapes=[pltpu.VMEM((MLA_HEADS, 1, tq), F32), pltpu.VMEM((MLA_HEADS, 1, tq), F32),
                        pltpu.VMEM((MLA_HEADS, kv_rank, tq), F32)],
        compiler_params=_cparams("parallel", "arbitrary"),
        name="mla_prompt_attention",
    )(qcat, kcat, ckvt, wuvt)


def _paged_kernel(pt_ref, q_ref, knew_ref, kpnew_ref, cache_ckv, cache_kpe_t, o_ref,
                  kc_buf, kp_buf, sem, *, n_batch, chunks_per_batch, pages_per_stream, page, dec_seq):
    total = n_batch * chunks_per_batch
    pages_per_chunk = PAGED_STREAMS * pages_per_stream
    kv_rank = kc_buf.shape[2]

    def page_copies(g, slot, p):
        pg = pt_ref[g * pages_per_chunk + p]
        rows = pl.ds(p * page, page)
        return (pltpu.make_async_copy(cache_ckv.at[0, pg], kc_buf.at[slot, rows, :], sem.at[0, slot]),
                pltpu.make_async_copy(cache_kpe_t.at[0, pg], kp_buf.at[slot, :, rows], sem.at[1, slot]))

    def start_chunk(g, slot):
        for p in range(pages_per_chunk):
            for cp in page_copies(g, slot, p):
                cp.start()

    def wait_chunk(g, slot):
        for p in range(pages_per_chunk):
            for cp in page_copies(g, slot, p):
                cp.wait()

    for g0 in range(min(PAGED_SLOTS - 1, total)):
        start_chunk(g0, g0)

    def batch_body(b, carry):
        qf = q_ref[b]
        qlf = qf[:, :kv_rank]
        qpf = qf[:, kv_rank:kv_rank + ROPE_DIM]
        qb = qlf.astype(BF16)
        qpb = qpf.astype(BF16)
        rows = qf.shape[0]
        stream_rows = pages_per_stream * page

        def chunk_body(c, states):
            g = b * chunks_per_batch + c
            slot = g % PAGED_SLOTS
            ahead = g + PAGED_SLOTS - 1

            @pl.when(ahead < total)
            def _():
                start_chunk(ahead, ahead % PAGED_SLOTS)

            wait_chunk(g, slot)
            keys, scores = [], []
            for st in range(PAGED_STREAMS):
                kc = kc_buf[slot, st * stream_rows:(st + 1) * stream_rows, :].astype(BF16)
                kp_t = kp_buf[slot, :, st * stream_rows:(st + 1) * stream_rows].astype(BF16)
                keys.append(kc)
                scores.append(_dot_nt(qb, kc) + _dot(qpb, kp_t))
            new_states = []
            for (m_prev, l_prev, acc), kc, s in zip(states, keys, scores):
                m_new = jnp.maximum(m_prev, jnp.max(s, axis=-1, keepdims=True))
                alpha = jnp.exp2(m_prev - m_new)
                p = jnp.exp2(s - m_new)
                l_new = alpha * l_prev + jnp.sum(p, axis=-1, keepdims=True)
                new_states.append((m_new, l_new, alpha * acc + _dot(p.astype(BF16), kc)))
            return tuple(new_states)

        init = tuple((jnp.full((rows, 1), -jnp.inf, F32), jnp.zeros((rows, 1), F32), jnp.zeros(qlf.shape, F32))
                     for _ in range(PAGED_STREAMS))
        states = lax.fori_loop(0, chunks_per_batch, chunk_body, init)

        knew = knew_ref[b]
        kpnew = kpnew_ref[b]
        tok = lax.broadcasted_iota(jnp.int32, (rows, 1), 0) % dec_seq
        s_new = []
        for j in range(dec_seq):
            sj = (jnp.sum(qlf * knew[j:j + 1, :], axis=-1, keepdims=True)
                  + jnp.sum(qpf * kpnew[j:j + 1, :], axis=-1, keepdims=True))
            s_new.append(jnp.where(tok >= j, sj, -jnp.inf))
        m_new = states[0][0]
        for m_st, _, _ in states[1:]:
            m_new = jnp.maximum(m_new, m_st)
        for sj in s_new:
            m_new = jnp.maximum(m_new, sj)
        l_new = jnp.zeros((rows, 1), F32)
        acc = jnp.zeros(qlf.shape, F32)
        for m_st, l_st, acc_st in states:
            alpha = jnp.exp2(m_st - m_new)
            l_new = l_new + alpha * l_st
            acc = acc + alpha * acc_st
        for j, sj in enumerate(s_new):
            pj = jnp.exp2(sj - m_new)
            l_new = l_new + pj
            acc = acc + pj * knew[j:j + 1, :]
        o_ref[b] = acc / l_new
        return carry

    lax.fori_loop(0, n_batch, batch_body, 0)


def _attn_paged(page_table, q3, knew3, kpnew3, cache_ckv, cache_kpe_t):
    n_batch, n_pages = page_table.shape
    page = cache_ckv.shape[2]
    kv_rank = cache_ckv.shape[3]
    dec_seq = knew3.shape[1]
    pps = MAX_PAGES_PER_STREAM
    while n_pages % (PAGED_STREAMS * pps):
        pps //= 2
    assert pps >= 1, n_pages
    ppc = PAGED_STREAMS * pps
    kernel = functools.partial(_paged_kernel, n_batch=n_batch, chunks_per_batch=n_pages // ppc,
                               pages_per_stream=pps, page=page, dec_seq=dec_seq)
    vmem = lambda shape: pl.BlockSpec(shape, lambda i, pt: (0,) * len(shape))
    any_spec = pl.BlockSpec(memory_space=pl.ANY)
    out_shape = (n_batch, q3.shape[1], kv_rank)
    return pl.pallas_call(
        kernel,
        out_shape=jax.ShapeDtypeStruct(out_shape, F32),
        grid_spec=pltpu.PrefetchScalarGridSpec(
            num_scalar_prefetch=1,
            grid=(1,),
            in_specs=[vmem(q3.shape), vmem(knew3.shape), vmem(kpnew3.shape), any_spec, any_spec],
            out_specs=vmem(out_shape),
            scratch_shapes=[pltpu.VMEM((PAGED_SLOTS, ppc * page, kv_rank), F32),
                            pltpu.VMEM((PAGED_SLOTS, cache_kpe_t.shape[2], ppc * page), F32),
                            pltpu.SemaphoreType.DMA((2, PAGED_SLOTS))]),
        compiler_params=_cparams("arbitrary"),
        name="mla_paged_attention",
    )(page_table.reshape(-1), q3, knew3, kpnew3, cache_ckv, cache_kpe_t)


def _uv_kernel(o_ref, wuv_ref, out_ref):
    kv_rank = wuv_ref.shape[1]
    outs = [_dot(o_ref[:, h * kv_rank:(h + 1) * kv_rank].astype(BF16), wuv_ref[h]) for h in range(MLA_HEADS)]
    out_ref[...] = jnp.concatenate(outs, axis=-1)


def _uv_proj(o_lat, wuv):
    t = o_lat.shape[0]
    return pl.pallas_call(
        _uv_kernel,
        out_shape=jax.ShapeDtypeStruct((t, MLA_HEADS * V_DIM), F32),
        grid=(1,),
        in_specs=[_const_spec(o_lat.shape), _const_spec(wuv.shape)],
        out_specs=_const_spec((t, MLA_HEADS * V_DIM)),
        compiler_params=_cparams("arbitrary"),
        name="mla_value_up_proj",
    )(o_lat, wuv)


def _softplus(z):
    return jnp.maximum(z, 0.0) + jnp.log(1.0 + jnp.exp(-jnp.abs(z)))


def _rwkv_prep_kernel(rw_ref, first_ref, *refs, seq, tiles_per_seq):
    x = rw_ref[...]
    row = lax.broadcasted_iota(jnp.int32, x.shape, 0)
    rolled = pltpu.roll(x, 1, 0)
    if tiles_per_seq:
        prev8_ref, refs = refs[0], refs[1:]
        at_start = pl.program_id(0) % tiles_per_seq == 0
        first = jnp.where(at_start, first_ref[...], prev8_ref[7:8, :])
        prev = jnp.where(row == 0, first, rolled)
    else:
        prev = jnp.where(row % seq == 0, first_ref[...], rolled)
    (mu_ref, w0_ref, a0_ref, wa_up_ref, g_up_ref, kk_ref, ka_ref, rk_ref, ones_ref,
     r_o, w_o, k_o, v_o, a_o, b_o, g_o, bonus_o) = refs
    xr = x + (prev - x) * mu_ref[...]
    d = RWKV_DIM
    r = xr[:, 0:d]
    k = xr[:, d:2 * d]
    v = xr[:, 2 * d:3 * d]
    xwa = xr[:, 3 * d:3 * d + W_LORA + A_LORA]
    xg = xr[:, 3 * d + W_LORA + A_LORA:]
    lane = lax.broadcasted_iota(jnp.int32, xwa.shape, 1)
    wa = _dot(jnp.where(lane < W_LORA, jnp.tanh(xwa), xwa).astype(BF16), wa_up_ref[...])
    w = -_softplus(-(w0_ref[...] + wa[:, :d])) - 0.5
    a = _sigmoid(a0_ref[...] + wa[:, d:])
    ones = ones_ref[...]
    kk = k * kk_ref[...]
    kk = kk / jnp.maximum(jnp.sqrt(_segsum(kk * kk, ones)), 1e-12)
    k2 = k * (1.0 + (a - 1.0) * ka_ref[...])
    scan_ops = ((r_o, r), (w_o, jnp.exp(-jnp.exp(w))), (k_o, k2), (v_o, v), (a_o, -kk), (b_o, kk * a))
    for o_ref, val in scan_ops:
        o_ref[...] = val.T if tiles_per_seq else val
    g_o[...] = _dot(_sigmoid(xg).astype(BF16), g_up_ref[...])
    bonus_o[...] = _segsum(r * k2 * rk_ref[...], ones) * v


def _rwkv_prep(rw, shift0, seq, mu, w0, a0, wa_up, g_up, k_k, k_a, r_k, ones):
    t, width = rw.shape
    tm = _row_tile(t, 512)
    consts = (mu, w0, a0, wa_up, g_up, k_k, k_a, r_k, ones)
    if seq % tm == 0:
        tiles_per_seq = seq // tm
        first = shift0.reshape(-1, 1, width)
        lead = [first, rw]
        lead_specs = [pl.BlockSpec((None, 1, width), lambda i: (i // tiles_per_seq, 0, 0)),
                      pl.BlockSpec((8, width), lambda i: (jnp.maximum(i * (tm // 8) - 1, 0), 0))]
    else:
        assert tm % seq == 0
        tiles_per_seq = 0
        lead = [jnp.repeat(shift0, seq, axis=0)]
        lead_specs = [_rows_spec(tm, width)]
    out_shape = [jax.ShapeDtypeStruct((t, RWKV_DIM), F32)] * 8
    out_specs = [_rows_spec(tm, RWKV_DIM)] * 8
    if tiles_per_seq:
        out_shape[:6] = [jax.ShapeDtypeStruct((t // seq, RWKV_DIM, seq), F32)] * 6
        out_specs[:6] = [pl.BlockSpec((None, RWKV_DIM, tm), lambda i: (i // tiles_per_seq, 0, i % tiles_per_seq))] * 6
    return pl.pallas_call(
        functools.partial(_rwkv_prep_kernel, seq=seq, tiles_per_seq=tiles_per_seq),
        out_shape=out_shape,
        grid=(t // tm,),
        in_specs=[_rows_spec(tm, width)] + lead_specs + [_const_spec(c.shape) for c in consts],
        out_specs=out_specs,
        compiler_params=_cparams("parallel"),
        name="rwkv_prep",
    )(rw, *lead, *consts)


def _scan_steps(st_ref, ts, row, value_at, put_y):
    group = RWKV_HEAD // SCAN_ROW_SPLIT
    groups = [slice(i * group, (i + 1) * group) for i in range(SCAN_ROW_SPLIT)]

    def state_dot(name, t, rows):
        acc = st_ref[0, rows, :] * row(name, t, 0)
        for kq in range(1, HALF_HEAD):
            acc = acc + st_ref[kq, rows, :] * row(name, t, kq)
        return acc

    def step(t, sa_halves):
        t_next = jnp.minimum(t + 1, ts - 1)
        out = []
        for rows, sa_half in zip(groups, sa_halves):
            sa = sa_half + pltpu.roll(sa_half, LANES // 2, 1)
            vt = value_at(t, rows)
            y = None
            sa_next = None
            for kq in range(HALF_HEAD):
                s_new = st_ref[kq, rows, :] * row("w", t, kq) + sa * row("b", t, kq) + vt * row("k", t, kq)
                st_ref[kq, rows, :] = s_new
                y_term = s_new * row("r", t, kq)
                a_term = s_new * row("a", t_next, kq)
                y = y_term if y is None else y + y_term
                sa_next = a_term if sa_next is None else sa_next + a_term
            put_y(t, rows, y + pltpu.roll(y, LANES // 2, 1))
            out.append(sa_next)
        return tuple(out)

    lax.fori_loop(0, ts, step, tuple(state_dot("a", 0, rows) for rows in groups))


def _scan_kernel(r_ref, w_ref, k_ref, a_ref, b_ref, v_ref, s0_ref, y_ref, sfin_ref, st_ref, *, ts):
    j = pl.program_id(1)

    @pl.when(j == 0)
    def _():
        st_ref[...] = s0_ref[...]

    refs = dict(r=r_ref, w=w_ref, k=k_ref, a=a_ref, b=b_ref)

    def put_y(t, rows, y):
        y_ref[t, rows, :] = y

    _scan_steps(st_ref, ts, lambda name, t, kq: refs[name][t, pl.ds(kq, 1), :], lambda t, rows: v_ref[t, rows, :], put_y)

    @pl.when(j == pl.num_programs(1) - 1)
    def _():
        sfin_ref[...] = st_ref[...]


def _scan_kernel_cm(r_ref, w_ref, k_ref, a_ref, b_ref, v_ref, s0_ref, y_ref, sfin_ref,
                    st_ref, r_s, w_s, k_s, a_s, b_s, v_s, y_s, *, ts):
    j = pl.program_id(1)

    @pl.when(j == 0)
    def _():
        st_ref[...] = s0_ref[...]

    def head_rows(ref, c0):
        return jnp.concatenate([ref[b, pl.ds(c0, RWKV_HEADS, stride=RWKV_HEAD), :] for b in range(SCAN_BATCH)], axis=0)

    for src, dst in ((r_ref, r_s), (w_ref, w_s), (k_ref, k_s), (a_ref, a_s), (b_ref, b_s)):
        def key_tile(kq, carry, src=src, dst=dst):
            tile = jnp.concatenate([head_rows(src, kq), head_rows(src, HALF_HEAD + kq)], axis=0)
            dst[kq, :ts, :] = tile.T
            return carry

        lax.fori_loop(0, HALF_HEAD, key_tile, 0, unroll=RELAYOUT_UNROLL)

    value_rows = lambda v: pl.ds(v, ts, stride=RWKV_HEAD)

    def value_tile(v, carry):
        half = head_rows(v_ref, v)
        v_s[value_rows(v), :] = jnp.concatenate([half, half], axis=0).T
        return carry

    lax.fori_loop(0, RWKV_HEAD, value_tile, 0, unroll=RELAYOUT_UNROLL)

    scr = dict(r=r_s, w=w_s, k=k_s, a=a_s, b=b_s)
    step_rows = lambda t, rows: pl.ds(pl.multiple_of(t * RWKV_HEAD, RWKV_HEAD) + rows.start, rows.stop - rows.start)

    def put_y(t, rows, y):
        y_s[step_rows(t, rows), :] = y

    _scan_steps(st_ref, ts, lambda name, t, kq: scr[name][kq, pl.ds(t, 1), :],
                lambda t, rows: v_s[step_rows(t, rows), :], put_y)

    def y_tile(v, carry):
        tile = y_s[value_rows(v), :].T
        for b in range(SCAN_BATCH):
            y_ref[b, pl.ds(v, RWKV_HEADS, stride=RWKV_HEAD), :] = tile[b * RWKV_HEADS:(b + 1) * RWKV_HEADS, :]
        return carry

    lax.fori_loop(0, RWKV_HEAD, y_tile, 0, unroll=RELAYOUT_UNROLL)

    @pl.when(j == pl.num_programs(1) - 1)
    def _():
        sfin_ref[...] = st_ref[...]


def _wkv_scan_cm(r, w, k, a, b, v, s0):
    batch, _, seq = r.shape
    ts = LANES
    assert seq % ts == 0 and batch % SCAN_BATCH == 0
    cm_spec = pl.BlockSpec((SCAN_BATCH, RWKV_DIM, ts), lambda g, j: (g, 0, j))
    state_spec = pl.BlockSpec((None, HALF_HEAD, RWKV_HEAD, LANES), lambda g, j: (g, 0, 0, 0))
    return pl.pallas_call(
        functools.partial(_scan_kernel_cm, ts=ts),
        out_shape=[jax.ShapeDtypeStruct(r.shape, F32), jax.ShapeDtypeStruct(s0.shape, F32)],
        grid=(batch // SCAN_BATCH, seq // ts),
        in_specs=[cm_spec] * 6 + [state_spec],
        out_specs=[cm_spec, state_spec],
        scratch_shapes=[pltpu.VMEM((HALF_HEAD, RWKV_HEAD, LANES), F32)]
                       + [pltpu.VMEM((HALF_HEAD, ts + SCRATCH_ROW_PAD, LANES), F32)] * 5
                       + [pltpu.VMEM((RWKV_HEAD * ts, LANES), F32)] * 2,
        compiler_params=_cparams("parallel", "arbitrary"),
        name="rwkv7_state_scan",
    )(r, w, k, a, b, v, s0)


def _wkv_scan(r, w, k, a, b, v, s0):
    ng, seq = r.shape[:2]
    ts = _row_tile(seq, 128) if seq % 8 == 0 else seq
    half_spec = pl.BlockSpec((None, ts, HALF_HEAD, LANES), lambda g, j: (g, j, 0, 0))
    full_spec = pl.BlockSpec((None, ts, RWKV_HEAD, LANES), lambda g, j: (g, j, 0, 0))
    state_spec = pl.BlockSpec((None, HALF_HEAD, RWKV_HEAD, LANES), lambda g, j: (g, 0, 0, 0))
    return pl.pallas_call(
        functools.partial(_scan_kernel, ts=ts),
        out_shape=[jax.ShapeDtypeStruct(v.shape, F32), jax.ShapeDtypeStruct(s0.shape, F32)],
        grid=(ng, seq // ts),
        in_specs=[half_spec] * 5 + [full_spec, state_spec],
        out_specs=[full_spec, state_spec],
        scratch_shapes=[pltpu.VMEM((HALF_HEAD, RWKV_HEAD, LANES), F32)],
        compiler_params=_cparams("parallel", "arbitrary"),
        name="rwkv7_state_scan",
    )(r, w, k, a, b, v, s0)


def _to_scan_layout(x, batch, seq, dup):
    ng = batch // SCAN_BATCH
    if dup:
        x = x.reshape(ng, SCAN_BATCH, seq, RWKV_HEADS, RWKV_HEAD).transpose(0, 2, 4, 1, 3)
        x = x.reshape(ng, seq, RWKV_HEAD, SCAN_BATCH * RWKV_HEADS)
        return jnp.concatenate([x, x], axis=-1)
    x = x.reshape(ng, SCAN_BATCH, seq, RWKV_HEADS, 2, HALF_HEAD).transpose(0, 2, 5, 4, 1, 3)
    return x.reshape(ng, seq, HALF_HEAD, LANES)


def _from_scan_layout(y, batch, seq):
    ng = batch // SCAN_BATCH
    y = y[..., :LANES // 2].reshape(ng, seq, RWKV_HEAD, SCAN_BATCH, RWKV_HEADS).transpose(0, 3, 1, 4, 2)
    return y.reshape(batch * seq, RWKV_DIM)


def _state_to_scan_layout(s, batch):
    ng = batch // SCAN_BATCH
    s = s.reshape(ng, SCAN_BATCH, RWKV_HEADS, RWKV_HEAD, 2, HALF_HEAD).transpose(0, 5, 3, 4, 1, 2)
    return s.reshape(ng, HALF_HEAD, RWKV_HEAD, LANES)


def _state_from_scan_layout(s, batch):
    ng = batch // SCAN_BATCH
    s = s.reshape(ng, HALF_HEAD, RWKV_HEAD, 2, SCAN_BATCH, RWKV_HEADS).transpose(0, 4, 5, 2, 3, 1)
    return s.reshape(batch, RWKV_HEADS, RWKV_HEAD, RWKV_HEAD)


def _merge_kernel(h_ref, y_ref, bonus_ref, g_ref, oa_ref, gates_ref, lnw_ref, lnb_ref, ones_ref, wob_ref, woa_ref,
                  wout_ref, post_ref, o_ref, *, y_channel_major):
    ones = ones_ref[...]
    y = y_ref[...].T if y_channel_major else y_ref[...]
    inv_n = 1.0 / RWKV_HEAD
    mu = _segsum(y, ones) * inv_n
    yc = y - mu
    var = _segsum(yc * yc, ones) * inv_n
    yn = yc * lax.rsqrt(var + GN_EPS) * lnw_ref[...] + lnb_ref[...] + bonus_ref[...]
    o_b = _dot((yn * g_ref[...]).astype(BF16), wob_ref[...])
    o_a = _dot(oa_ref[...].astype(BF16), woa_ref[...])
    d = o_ref.shape[1]
    gates = gates_ref[...]
    mix = _sigmoid(gates[:, :d]) * o_a + _sigmoid(gates[:, d:]) * o_b
    o_ref[...] = h_ref[...] + _rms(_dot(mix.astype(BF16), wout_ref[...]), post_ref[...])


def _merge(h, y, bonus, g, oa, gates, lnw, lnb, ones, wob, woa, wout, post_g):
    t, d = h.shape
    tm = _row_tile(t, 512)
    consts = (lnw, lnb, ones, wob, woa, wout, post_g)
    y_spec = _rows_spec(tm, RWKV_DIM)
    if y.ndim == 3:
        tiles_per_seq = y.shape[2] // tm
        y_spec = pl.BlockSpec((None, RWKV_DIM, tm), lambda i: (i // tiles_per_seq, 0, i % tiles_per_seq))
    return pl.pallas_call(
        functools.partial(_merge_kernel, y_channel_major=y.ndim == 3),
        out_shape=jax.ShapeDtypeStruct((t, d), F32),
        grid=(t // tm,),
        in_specs=[_rows_spec(tm, d), y_spec] + [_rows_spec(tm, RWKV_DIM)] * 3 + [_rows_spec(tm, 2 * d)]
                 + [_const_spec(c.shape) for c in consts],
        out_specs=_rows_spec(tm, d),
        compiler_params=_cparams("parallel"),
        name="gated_merge_out_proj",
    )(h, y, bonus, g, oa, gates, *consts)


def _ple_kernel(h_ref, p_ref, pre_ref, wgate_ref, wproj_ref, post_ref, o_ref):
    h = h_ref[...]
    gp = _sigmoid(_dot(_rms(h, pre_ref[...]).astype(BF16), wgate_ref[...]))
    emb = _dot(p_ref[...].astype(BF16), wproj_ref[...])
    o_ref[...] = h + _rms(gp * emb, post_ref[...])


def _ple(h, p, pre_g, wgate, wproj, post_g):
    t, d = h.shape
    tm = _row_tile(t, 512)
    consts = (pre_g, wgate, wproj, post_g)
    return pl.pallas_call(
        _ple_kernel,
        out_shape=jax.ShapeDtypeStruct((t, d), F32),
        grid=(t // tm,),
        in_specs=[_rows_spec(tm, d), _rows_spec(tm, p.shape[1])] + [_const_spec(c.shape) for c in consts],
        out_specs=_rows_spec(tm, d),
        compiler_params=_cparams("parallel"),
        name="per_layer_embedding",
    )(h, p, *consts)


def _rope_tables(pos):
    inv = ROPE_THETA ** (-jnp.arange(0, ROPE_DIM, 2, dtype=F32) / ROPE_DIM)
    ang = pos[:, None] * inv[None, :]
    cos, sin = jnp.cos(ang), jnp.sin(ang)
    reps = ROPE_PAD // ROPE_DIM
    return jnp.tile(jnp.concatenate([cos, cos], axis=-1), (1, reps)), jnp.tile(jnp.concatenate([-sin, sin], axis=-1), (1, reps))


def _row(v):
    return v.reshape(1, -1).astype(F32)


def kernel(x_prompt, x_sample, p_prompt, p_sample, cache_ckv, cache_kpe, state_wkv, state_shift, page_table, ffn1_pre_g, ffn1_wg, ffn1_wu, ffn1_wd, ffn1_post_g, mix_pre_g, w_in, q_norm_g, w_uq, kv_norm_g, w_uk, w_uv, w_oa, rwkv_mu, rwkv_w0, rwkv_w_up, rwkv_a0, rwkv_a_up, rwkv_g_up, rwkv_k_k, rwkv_k_a, rwkv_r_k, rwkv_ln_w, rwkv_ln_b, w_ob, w_out, mix_post_g, ffn2_pre_g, ffn2_wg, ffn2_wu, ffn2_wd, ffn2_post_g, ple_pre_g, w_ple_gate, w_ple_proj, ple_post_g):
    depth = w_in.shape[0]
    assert depth == 1, "single-layer step"
    batch, seq, d_model = x_prompt.shape
    dec_batch, dec_seq, _ = x_sample.shape
    n_pages = page_table.shape[1]
    page = cache_ckv.shape[2]
    kv_rank = cache_ckv.shape[3]
    q_rank = w_uq.shape[1]
    shift_dim = state_shift.shape[2]
    tp, ts_tok = batch * seq, dec_batch * dec_seq
    assert batch % SCAN_BATCH == 0 and dec_batch % SCAN_BATCH == 0
    assert shift_dim == 3 * RWKV_DIM + W_LORA + A_LORA + G_LORA
    bf = lambda w: w.astype(BF16)

    w_in0 = w_in[0]
    c0, c1, c2, c3 = q_rank, q_rank + kv_rank, q_rank + kv_rank + ROPE_DIM, q_rank + kv_rank + ROPE_DIM + shift_dim
    wq, wkv, wrw, wgate = bf(w_in0[:, :c0]), bf(w_in0[:, c0:c1]), bf(w_in0[:, c2:c3]), bf(w_in0[:, c3:])
    wkpe = bf(jnp.pad(w_in0[:, c1:c2], ((0, 0), (0, ROPE_PAD - ROPE_DIM))))
    wuq3 = w_uq[0].reshape(q_rank, MLA_HEADS, NOPE_DIM + ROPE_DIM)
    wuq_nope = wuq3[:, :, :NOPE_DIM].reshape(q_rank, MLA_HEADS * NOPE_DIM)
    wuq_pe = jnp.pad(wuq3[:, :, NOPE_DIM:], ((0, 0), (0, 0), (0, ROPE_PAD - ROPE_DIM))).reshape(q_rank, MLA_HEADS * ROPE_PAD)
    wuq = bf(jnp.concatenate([wuq_nope, wuq_pe], axis=1))
    wukt = bf(w_uk[0].transpose(1, 2, 0))
    wuv = bf(w_uv[0].transpose(1, 0, 2))
    wuvt = bf(w_uv[0].transpose(1, 2, 0))
    wa_up = jnp.zeros((W_LORA + A_LORA, 2 * RWKV_DIM), F32)
    wa_up = bf(wa_up.at[:W_LORA, :RWKV_DIM].set(rwkv_w_up[0]).at[W_LORA:, RWKV_DIM:].set(rwkv_a_up[0]))
    seg = jnp.arange(RWKV_DIM) // RWKV_HEAD
    ones = (seg[:, None] == seg[None, :]).astype(BF16)
    ffn1 = (_row(ffn1_pre_g), bf(ffn1_wg[0]), bf(ffn1_wu[0]), bf(ffn1_wd[0]), _row(ffn1_post_g))
    ffn2 = (_row(ffn2_pre_g), bf(ffn2_wg[0]), bf(ffn2_wu[0]), bf(ffn2_wd[0]), _row(ffn2_post_g))
    inproj_w = (_row(mix_pre_g), wq, wkv, wkpe, wrw, wgate, _row(q_norm_g), wuq, _row(kv_norm_g), wukt)
    prep_w = (_row(rwkv_mu), _row(rwkv_w0), _row(rwkv_a0), wa_up, bf(rwkv_g_up[0]), _row(rwkv_k_k), _row(rwkv_k_a),
              _row(rwkv_r_k), ones)
    merge_w = (_row(rwkv_ln_w), _row(rwkv_ln_b), ones, bf(w_ob[0]), bf(w_oa[0]), bf(w_out[0]), _row(mix_post_g))
    ple_w = (_row(ple_pre_g), bf(w_ple_gate[0]), bf(w_ple_proj[0]), _row(ple_post_g))

    def rwkv_mix(rw, shift0, s0, nb, ns):
        r, w, k, v, a, b, g, bonus = _rwkv_prep(rw, shift0, ns, *prep_w)
        if r.ndim == 3:
            y, s_fin = _wkv_scan_cm(r, w, k, a, b, v, _state_to_scan_layout(s0, nb))
            return y, bonus, g, _state_from_scan_layout(s_fin, nb)
        halves = [_to_scan_layout(t_, nb, ns, False) for t_ in (r, w, k, a, b)]
        y, s_fin = _wkv_scan(*halves, _to_scan_layout(v, nb, ns, True), _state_to_scan_layout(s0, nb))
        return _from_scan_layout(y, nb, ns), bonus, g, _state_from_scan_layout(s_fin, nb)

    tq = _row_tile(seq, 256)
    cos_p, sin_p = _rope_tables(jnp.arange(seq, dtype=F32))
    h1 = _ffn(x_prompt.reshape(tp, d_model), *ffn1)
    qcat, ckv_p, kpe_p, rw_p, gates, kcat, ckvt = _inproj(h1, *inproj_w, cos_p, sin_p, tq)
    oa = _attn_prompt(qcat, kcat, ckvt, wuvt, batch, seq, tq)
    y, bonus, g, wkv_p = rwkv_mix(rw_p, jnp.zeros((batch, shift_dim), F32),
                                  jnp.zeros((batch, RWKV_HEADS, RWKV_HEAD, RWKV_HEAD), F32), batch, seq)
    h2 = _merge(h1, y, bonus, g, oa, gates, *merge_w)
    h3 = _ffn(h2, *ffn2)
    y_prompt = _ple(h3, p_prompt[0].reshape(tp, -1), *ple_w)

    pos_s = jnp.tile(n_pages * page + jnp.arange(dec_seq, dtype=F32), dec_batch)
    cos_s, sin_s = _rope_tables(pos_s)
    h1 = _ffn(x_sample.reshape(ts_tok, d_model), *ffn1)
    qcat, ckv_s, kpe_s, rw_s, gates = _inproj(h1, *inproj_w, cos_s, sin_s, 0)
    qw = kv_rank + ROPE_PAD
    q3 = qcat.reshape(dec_batch, dec_seq, MLA_HEADS, qw).transpose(0, 2, 1, 3).reshape(
        dec_batch, MLA_HEADS * dec_seq, qw).astype(F32)
    o3 = _attn_paged(page_table, q3, ckv_s.reshape(dec_batch, dec_seq, kv_rank),
                     kpe_s.reshape(dec_batch, dec_seq, ROPE_DIM), cache_ckv, jnp.swapaxes(cache_kpe, 2, 3))
    o_lat = o3.reshape(dec_batch, MLA_HEADS, dec_seq, kv_rank).transpose(0, 2, 1, 3).reshape(ts_tok, MLA_HEADS * kv_rank)
    oa = _uv_proj(o_lat, wuv)
    y, bonus, g, wkv_s = rwkv_mix(rw_s, state_shift[0], state_wkv[0].astype(F32), dec_batch, dec_seq)
    h2 = _merge(h1, y, bonus, g, oa, gates, *merge_w)
    h3 = _ffn(h2, *ffn2)
    y_sample = _ple(h3, p_sample[0].reshape(ts_tok, -1), *ple_w)

    return (y_prompt.reshape(batch, seq, d_model), y_sample.reshape(dec_batch, dec_seq, d_model),
            ckv_p.reshape(1, batch, seq, kv_rank), kpe_p.reshape(1, batch, seq, ROPE_DIM),
            wkv_p[None], rw_p.reshape(batch, seq, shift_dim)[:, -1][None],
            ckv_s.reshape(1, dec_batch, dec_seq, kv_rank), kpe_s.reshape(1, dec_batch, dec_seq, ROPE_DIM),
            wkv_s[None], rw_s.reshape(dec_batch, dec_seq, shift_dim)[:, -1][None])
```

```python
import functools
import math

import jax
import jax.numpy as jnp
from jax import lax
from jax.experimental import pallas as pl
from jax.experimental.pallas import tpu as pltpu

F32 = jnp.float32
BF16 = jnp.bfloat16

MLA_HEADS = 8
NOPE_DIM = 64
ROPE_DIM = 32
V_DIM = 64
ROPE_THETA = 10000.0
SOFTMAX_SCALE = (NOPE_DIM + ROPE_DIM) ** -0.5
Q_SCALE = SOFTMAX_SCALE * math.log2(math.e)
RWKV_HEADS = 8
RWKV_HEAD = 64
RWKV_DIM = RWKV_HEADS * RWKV_HEAD
W_LORA = 64
A_LORA = 64
G_LORA = 128
GN_EPS = 64e-5
NORM_EPS = 1e-6

LANES = 128
VMEM_LIMIT_BYTES = 56 * 1024 * 1024

ROPE_PAD = LANES
SCAN_BATCH = 8
HALF_HEAD = RWKV_HEAD // 2
SCAN_ROW_SPLIT = 2
RELAYOUT_UNROLL = 8
SCRATCH_ROW_PAD = 8
MAX_PAGES_PER_STREAM = 16
PAGED_STREAMS = 2
PAGED_SLOTS = 3
SCORE_LOOKAHEAD = 8


def _cparams(*sem):
    return pltpu.CompilerParams(dimension_semantics=sem, vmem_limit_bytes=VMEM_LIMIT_BYTES)


def _dot(a, b):
    return jnp.dot(a, b, preferred_element_type=F32)


def _dot_nt(a, b):
    return lax.dot_general(a, b, (((1,), (1,)), ((), ())), preferred_element_type=F32)


def _rms(x, g):
    return x * lax.rsqrt(jnp.mean(x * x, axis=-1, keepdims=True) + NORM_EPS) * g


def _sigmoid(x):
    return 1.0 / (1.0 + jnp.exp(-x))


def _rope(x, cos, sin_signed):
    lane = lax.broadcasted_iota(jnp.int32, x.shape, 1)
    swapped = jnp.where((lane % ROPE_DIM) < ROPE_DIM // 2,
                        pltpu.roll(x, LANES - ROPE_DIM // 2, 1), pltpu.roll(x, ROPE_DIM // 2, 1))
    return x * cos + swapped * sin_signed


def _segsum(x, ones_bf16):
    hi = x.astype(BF16)
    lo = (x - hi.astype(F32)).astype(BF16)
    return _dot(hi, ones_bf16) + _dot(lo, ones_bf16)


def _row_tile(n, cap):
    t = cap
    while n % t:
        t //= 2
    assert t >= 8, (n, cap)
    return t


def _const_spec(shape):
    nd = len(shape)
    return pl.BlockSpec(shape, lambda *_: (0,) * nd)


def _rows_spec(tm, width):
    return pl.BlockSpec((tm, width), lambda i: (i, 0))


def _ffn_kernel(x_ref, pre_ref, wg_ref, wu_ref, wd_ref, post_ref, o_ref, *, n_chunks):
    x = x_ref[...]
    xn = _rms(x, pre_ref[...]).astype(BF16)
    fc = wg_ref.shape[1] // n_chunks
    acc = None
    for c in range(n_chunks):
        g = _dot(xn, wg_ref[:, c * fc:(c + 1) * fc])
        u = _dot(xn, wu_ref[:, c * fc:(c + 1) * fc])
        a = (g * _sigmoid(g) * u).astype(BF16)
        d = _dot(a, wd_ref[c * fc:(c + 1) * fc, :])
        acc = d if acc is None else acc + d
    o_ref[...] = x + 0.5 * _rms(acc, post_ref[...])


def _ffn(x, pre_g, wg, wu, wd, post_g):
    t, d = x.shape
    f = wg.shape[1]
    tm = _row_tile(t, 512)
    n_chunks = 2 if f % (2 * LANES) == 0 else 1
    return pl.pallas_call(
        functools.partial(_ffn_kernel, n_chunks=n_chunks),
        out_shape=jax.ShapeDtypeStruct((t, d), F32),
        grid=(t // tm,),
        in_specs=[_rows_spec(tm, d), _const_spec((1, d)), _const_spec((d, f)), _const_spec((d, f)),
                  _const_spec((f, d)), _const_spec((1, d))],
        out_specs=_rows_spec(tm, d),
        compiler_params=_cparams("parallel"),
        name="ffn_half_step",
    )(x, pre_g, wg, wu, wd, post_g)


def _inproj_kernel(h_ref, g_ref, wq_ref, wkv_ref, wkpe_ref, wrw_ref, wgate_ref, qng_ref, wuq_ref, kvg_ref,
                   wukt_ref, cos_ref, sin_ref, qcat_ref, ckv_ref, kpe_ref, rw_ref, gates_ref, *kt_refs, tk):
    u = _rms(h_ref[...], g_ref[...]).astype(BF16)
    rw_ref[...] = _dot(u, wrw_ref[...])
    gates_ref[...] = _dot(u, wgate_ref[...])
    ckv = _rms(_dot(u, wkv_ref[...]), kvg_ref[...])
    ckv_ref[...] = ckv
    cos = cos_ref[...]
    sin = sin_ref[...]
    kpe = _rope(_dot(u, wkpe_ref[...]), cos, sin)
    kpe_ref[...] = kpe[:, :ROPE_DIM]
    kv_rank = ckv.shape[1]
    if kt_refs:
        kcat_ref, ckvt_ref = kt_refs
        kcat_ref[:, :kv_rank] = ckv.astype(BF16)
        kcat_ref[:, kv_rank:] = kpe.astype(BF16)
        for j in range(ckv.shape[0] // tk):
            ckvt_ref[j] = ckv[j * tk:(j + 1) * tk, :].T.astype(BF16)
    cq = _rms(_dot(u, wq_ref[...]), qng_ref[...]).astype(BF16)
    q = _dot(cq, wuq_ref[...])
    nope_w = MLA_HEADS * NOPE_DIM
    qw = kv_rank + ROPE_PAD
    for h in range(MLA_HEADS):
        nope = q[:, h * NOPE_DIM:(h + 1) * NOPE_DIM].astype(BF16)
        qcat_ref[:, h * qw:h * qw + kv_rank] = (_dot(nope, wukt_ref[h]) * Q_SCALE).astype(BF16)
        pe = q[:, nope_w + h * ROPE_PAD: nope_w + (h + 1) * ROPE_PAD]
        qcat_ref[:, h * qw + kv_rank:(h + 1) * qw] = (_rope(pe, cos, sin) * Q_SCALE).astype(BF16)


def _inproj(h, g, wq, wkv, wkpe, wrw, wgate, qng, wuq, kvg, wukt, cos, sin, tk):
    t, d = h.shape
    tm = _row_tile(t, 512)
    kv_rank = wkv.shape[1]
    qw = kv_rank + ROPE_PAD
    n_pos = cos.shape[0] // tm
    out_shape = [jax.ShapeDtypeStruct((t, MLA_HEADS * qw), BF16), jax.ShapeDtypeStruct((t, kv_rank), F32),
                 jax.ShapeDtypeStruct((t, ROPE_DIM), F32), jax.ShapeDtypeStruct((t, wrw.shape[1]), F32),
                 jax.ShapeDtypeStruct((t, wgate.shape[1]), F32)]
    out_specs = [_rows_spec(tm, s.shape[1]) for s in out_shape]
    if tk:
        assert tm % tk == 0
        out_shape += [jax.ShapeDtypeStruct((t, qw), BF16), jax.ShapeDtypeStruct((t // tk, kv_rank, tk), BF16)]
        out_specs += [_rows_spec(tm, qw), pl.BlockSpec((tm // tk, kv_rank, tk), lambda i: (i, 0, 0))]
    consts = (g, wq, wkv, wkpe, wrw, wgate, qng, wuq, kvg, wukt)
    pos_spec = pl.BlockSpec((tm, ROPE_PAD), lambda i: (i % n_pos, 0))
    return pl.pallas_call(
        functools.partial(_inproj_kernel, tk=tk),
        out_shape=out_shape,
        grid=(t // tm,),
        in_specs=[_rows_spec(tm, d)] + [_const_spec(c.shape) for c in consts] + [pos_spec, pos_spec],
        out_specs=out_specs,
        compiler_params=_cparams("parallel"),
        name="mixer_in_proj",
    )(h, *consts, cos, sin)


def _attn_kernel(qcat_ref, kcat_ref, ckvt_ref, wuvt_ref, o_ref, m_scr, l_scr, acc_scr, *, tq):
    i = pl.program_id(1)
    qw = kcat_ref.shape[1]
    m_scr[...] = jnp.full(m_scr.shape, -jnp.inf, F32)
    l_scr[...] = jnp.zeros(l_scr.shape, F32)
    acc_scr[...] = jnp.zeros(acc_scr.shape, F32)

    def block(kb, diagonal):
        off = pl.multiple_of(kb * tq, tq)
        kcat = kcat_ref[pl.ds(off, tq), :]
        vt = ckvt_ref[kb]
        def scores(h):
            return _dot_nt(kcat, qcat_ref[:, h * qw:(h + 1) * qw])

        pending = [scores(h) for h in range(min(SCORE_LOOKAHEAD, MLA_HEADS))]
        for h in range(MLA_HEADS):
            if h + SCORE_LOOKAHEAD < MLA_HEADS:
                pending.append(scores(h + SCORE_LOOKAHEAD))
            s = pending.pop(0)
            if diagonal:
                key = lax.broadcasted_iota(jnp.int32, s.shape, 0)
                qry = lax.broadcasted_iota(jnp.int32, s.shape, 1)
                s = jnp.where(key <= qry, s, -jnp.inf)
            m_prev = m_scr[h]
            m_new = jnp.maximum(m_prev, jnp.max(s, axis=0, keepdims=True))
            alpha = jnp.exp2(m_prev - m_new)
            p = jnp.exp2(s - m_new)
            l_scr[h] = alpha * l_scr[h] + jnp.sum(p, axis=0, keepdims=True)
            acc_scr[h] = alpha * acc_scr[h] + _dot(vt, p.astype(BF16))
            m_scr[h] = m_new

    def off_diagonal(kb, carry):
        block(kb, False)
        return carry

    lax.fori_loop(0, i, off_diagonal, 0)
    block(i, True)
    outs = [_dot(wuvt_ref[h], (acc_scr[h] / l_scr[h]).astype(BF16)) for h in range(MLA_HEADS)]
    o_ref[...] = jnp.concatenate(outs, axis=0).T


def _attn_prompt(qcat, kcat, ckvt, wuvt, batch, seq, tq):
    qw = kcat.shape[1]
    kv_rank = ckvt.shape[1]
    nq = seq // tq
    return pl.pallas_call(
        functools.partial(_attn_kernel, tq=tq),
        out_shape=jax.ShapeDtypeStruct((batch * seq, MLA_HEADS * V_DIM), F32),
        grid=(batch, nq),
        in_specs=[pl.BlockSpec((tq, MLA_HEADS * qw), lambda b, i: (b * nq + i, 0)),
                  pl.BlockSpec((seq, qw), lambda b, i: (b, 0)),
                  pl.BlockSpec((nq, kv_rank, tq), lambda b, i: (b, 0, 0)),
                  _const_spec(wuvt.shape)],
        out_specs=pl.BlockSpec((tq, MLA_HEADS * V_DIM), lambda b, i: (b * nq + i, 0)),
        scratch_shapes=[pltpu.VMEM((MLA_HEADS, 1, tq), F32), pltpu.VMEM((MLA_HEADS, 1, tq), F32),
                        pltpu.VMEM((MLA_HEADS, kv_rank, tq), F32)],
        compiler_params=_cparams("parallel", "arbitrary"),
        name="mla_prompt_attention",
    )(qcat, kcat, ckvt, wuvt)


def _paged_kernel(pt_ref, q_ref, knew_ref, kpnew_ref, cache_ckv, cache_kpe_t, o_ref,
                  kc_buf, kp_buf, sem, *, n_batch, chunks_per_batch, pages_per_stream, page, dec_seq):
    total = n_batch * chunks_per_batch
    pages_per_chunk = PAGED_STREAMS * pages_per_stream
    kv_rank = kc_buf.shape[2]

    def page_copies(g, slot, p):
        pg = pt_ref[g * pages_per_chunk + p]
        rows = pl.ds(p * page, page)
        return (pltpu.make_async_copy(cache_ckv.at[0, pg], kc_buf.at[slot, rows, :], sem.at[0, slot]),
                pltpu.make_async_copy(cache_kpe_t.at[0, pg], kp_buf.at[slot, :, rows], sem.at[1, slot]))

    def start_chunk(g, slot):
        for p in range(pages_per_chunk):
            for cp in page_copies(g, slot, p):
                cp.start()

    def wait_chunk(g, slot):
        for p in range(pages_per_chunk):
            for cp in page_copies(g, slot, p):
                cp.wait()

    for g0 in range(min(PAGED_SLOTS - 1, total)):
        start_chunk(g0, g0)

    def batch_body(b, carry):
        qf = q_ref[b]
        qlf = qf[:, :kv_rank]
        qpf = qf[:, kv_rank:kv_rank + ROPE_DIM]
        qb = qlf.astype(BF16)
        qpb = qpf.astype(BF16)
        rows = qf.shape[0]
        stream_rows = pages_per_stream * page

        def chunk_body(c, states):
            g = b * chunks_per_batch + c
            slot = g % PAGED_SLOTS
            ahead = g + PAGED_SLOTS - 1

            @pl.when(ahead < total)
            def _():
                start_chunk(ahead, ahead % PAGED_SLOTS)

            wait_chunk(g, slot)
            keys, scores = [], []
            for st in range(PAGED_STREAMS):
                kc = kc_buf[slot, st * stream_rows:(st + 1) * stream_rows, :].astype(BF16)
                kp_t = kp_buf[slot, :, st * stream_rows:(st + 1) * stream_rows].astype(BF16)
                keys.append(kc)
                scores.append(_dot_nt(qb, kc) + _dot(qpb, kp_t))
            new_states = []
            for (m_prev, l_prev, acc), kc, s in zip(states, keys, scores):
                m_new = jnp.maximum(m_prev, jnp.max(s, axis=-1, keepdims=True))
                alpha = jnp.exp2(m_prev - m_new)
                p = jnp.exp2(s - m_new)
                l_new = alpha * l_prev + jnp.sum(p, axis=-1, keepdims=True)
                new_states.append((m_new, l_new, alpha * acc + _dot(p.astype(BF16), kc)))
            return tuple(new_states)

        init = tuple((jnp.full((rows, 1), -jnp.inf, F32), jnp.zeros((rows, 1), F32), jnp.zeros(qlf.shape, F32))
                     for _ in range(PAGED_STREAMS))
        states = lax.fori_loop(0, chunks_per_batch, chunk_body, init)

        knew = knew_ref[b]
        kpnew = kpnew_ref[b]
        tok = lax.broadcasted_iota(jnp.int32, (rows, 1), 0) % dec_seq
        s_new = []
        for j in range(dec_seq):
            sj = (jnp.sum(qlf * knew[j:j + 1, :], axis=-1, keepdims=True)
                  + jnp.sum(qpf * kpnew[j:j + 1, :], axis=-1, keepdims=True))
            s_new.append(jnp.where(tok >= j, sj, -jnp.inf))
        m_new = states[0][0]
        for m_st, _, _ in states[1:]:
            m_new = jnp.maximum(m_new, m_st)
        for sj in s_new:
            m_new = jnp.maximum(m_new, sj)
        l_new = jnp.zeros((rows, 1), F32)
        acc = jnp.zeros(qlf.shape, F32)
        for m_st, l_st, acc_st in states:
            alpha = jnp.exp2(m_st - m_new)
            l_new = l_new + alpha * l_st
            acc = acc + alpha * acc_st
        for j, sj in enumerate(s_new):
            pj = jnp.exp2(sj - m_new)
            l_new = l_new + pj
            acc = acc + pj * knew[j:j + 1, :]
        o_ref[b] = acc / l_new
        return carry

    lax.fori_loop(0, n_batch, batch_body, 0)


def _attn_paged(page_table, q3, knew3, kpnew3, cache_ckv, cache_kpe_t):
    n_batch, n_pages = page_table.shape
    page = cache_ckv.shape[2]
    kv_rank = cache_ckv.shape[3]
    dec_seq = knew3.shape[1]
    pps = MAX_PAGES_PER_STREAM
    while n_pages % (PAGED_STREAMS * pps):
        pps //= 2
    assert pps >= 1, n_pages
    ppc = PAGED_STREAMS * pps
    kernel = functools.partial(_paged_kernel, n_batch=n_batch, chunks_per_batch=n_pages // ppc,
                               pages_per_stream=pps, page=page, dec_seq=dec_seq)
    vmem = lambda shape: pl.BlockSpec(shape, lambda i, pt: (0,) * len(shape))
    any_spec = pl.BlockSpec(memory_space=pl.ANY)
    out_shape = (n_batch, q3.shape[1], kv_rank)
    return pl.pallas_call(
        kernel,
        out_shape=jax.ShapeDtypeStruct(out_shape, F32),
        grid_spec=pltpu.PrefetchScalarGridSpec(
            num_scalar_prefetch=1,
            grid=(1,),
            in_specs=[vmem(q3.shape), vmem(knew3.shape), vmem(kpnew3.shape), any_spec, any_spec],
            out_specs=vmem(out_shape),
            scratch_shapes=[pltpu.VMEM((PAGED_SLOTS, ppc * page, kv_rank), F32),
                            pltpu.VMEM((PAGED_SLOTS, cache_kpe_t.shape[2], ppc * page), F32),
                            pltpu.SemaphoreType.DMA((2, PAGED_SLOTS))]),
        compiler_params=_cparams("arbitrary"),
        name="mla_paged_attention",
    )(page_table.reshape(-1), q3, knew3, kpnew3, cache_ckv, cache_kpe_t)


def _uv_kernel(o_ref, wuv_ref, out_ref):
    kv_rank = wuv_ref.shape[1]
    outs = [_dot(o_ref[:, h * kv_rank:(h + 1) * kv_rank].astype(BF16), wuv_ref[h]) for h in range(MLA_HEADS)]
    out_ref[...] = jnp.concatenate(outs, axis=-1)


def _uv_proj(o_lat, wuv):
    t = o_lat.shape[0]
    return pl.pallas_call(
        _uv_kernel,
        out_shape=jax.ShapeDtypeStruct((t, MLA_HEADS * V_DIM), F32),
        grid=(1,),
        in_specs=[_const_spec(o_lat.shape), _const_spec(wuv.shape)],
        out_specs=_const_spec((t, MLA_HEADS * V_DIM)),
        compiler_params=_cparams("arbitrary"),
        name="mla_value_up_proj",
    )(o_lat, wuv)


def _softplus(z):
    return jnp.maximum(z, 0.0) + jnp.log(1.0 + jnp.exp(-jnp.abs(z)))


def _rwkv_prep_kernel(rw_ref, first_ref, *refs, seq, tiles_per_seq):
    x = rw_ref[...]
    row = lax.broadcasted_iota(jnp.int32, x.shape, 0)
    rolled = pltpu.roll(x, 1, 0)
    if tiles_per_seq:
        prev8_ref, refs = refs[0], refs[1:]
        at_start = pl.program_id(0) % tiles_per_seq == 0
        first = jnp.where(at_start, first_ref[...], prev8_ref[7:8, :])
        prev = jnp.where(row == 0, first, rolled)
    else:
        prev = jnp.where(row % seq == 0, first_ref[...], rolled)
    (mu_ref, w0_ref, a0_ref, wa_up_ref, g_up_ref, kk_ref, ka_ref, rk_ref, ones_ref,
     r_o, w_o, k_o, v_o, a_o, b_o, g_o, bonus_o) = refs
    xr = x + (prev - x) * mu_ref[...]
    d = RWKV_DIM
    r = xr[:, 0:d]
    k = xr[:, d:2 * d]
    v = xr[:, 2 * d:3 * d]
    xwa = xr[:, 3 * d:3 * d + W_LORA + A_LORA]
    xg = xr[:, 3 * d + W_LORA + A_LORA:]
    lane = lax.broadcasted_iota(jnp.int32, xwa.shape, 1)
    wa = _dot(jnp.where(lane < W_LORA, jnp.tanh(xwa), xwa).astype(BF16), wa_up_ref[...])
    w = -_softplus(-(w0_ref[...] + wa[:, :d])) - 0.5
    a = _sigmoid(a0_ref[...] + wa[:, d:])
    ones = ones_ref[...]
    kk = k * kk_ref[...]
    kk = kk / jnp.maximum(jnp.sqrt(_segsum(kk * kk, ones)), 1e-12)
    k2 = k * (1.0 + (a - 1.0) * ka_ref[...])
    scan_ops = ((r_o, r), (w_o, jnp.exp(-jnp.exp(w))), (k_o, k2), (v_o, v), (a_o, -kk), (b_o, kk * a))
    for o_ref, val in scan_ops:
        o_ref[...] = val.T if tiles_per_seq else val
    g_o[...] = _dot(_sigmoid(xg).astype(BF16), g_up_ref[...])
    bonus_o[...] = _segsum(r * k2 * rk_ref[...], ones) * v


def _rwkv_prep(rw, shift0, seq, mu, w0, a0, wa_up, g_up, k_k, k_a, r_k, ones):
    t, width = rw.shape
    tm = _row_tile(t, 512)
    consts = (mu, w0, a0, wa_up, g_up, k_k, k_a, r_k, ones)
    if seq % tm == 0:
        tiles_per_seq = seq // tm
        first = shift0.reshape(-1, 1, width)
        lead = [first, rw]
        lead_specs = [pl.BlockSpec((None, 1, width), lambda i: (i // tiles_per_seq, 0, 0)),
                      pl.BlockSpec((8, width), lambda i: (jnp.maximum(i * (tm // 8) - 1, 0), 0))]
    else:
        assert tm % seq == 0
        tiles_per_seq = 0
        lead = [jnp.repeat(shift0, seq, axis=0)]
        lead_specs = [_rows_spec(tm, width)]
    out_shape = [jax.ShapeDtypeStruct((t, RWKV_DIM), F32)] * 8
    out_specs = [_rows_spec(tm, RWKV_DIM)] * 8
    if tiles_per_seq:
        out_shape[:6] = [jax.ShapeDtypeStruct((t // seq, RWKV_DIM, seq), F32)] * 6
        out_specs[:6] = [pl.BlockSpec((None, RWKV_DIM, tm), lambda i: (i // tiles_per_seq, 0, i % tiles_per_seq))] * 6
    return pl.pallas_call(
        functools.partial(_rwkv_prep_kernel, seq=seq, tiles_per_seq=tiles_per_seq),
        out_shape=out_shape,
        grid=(t // tm,),
        in_specs=[_rows_spec(tm, width)] + lead_specs + [_const_spec(c.shape) for c in consts],
        out_specs=out_specs,
        compiler_params=_cparams("parallel"),
        name="rwkv_prep",
    )(rw, *lead, *consts)


def _scan_steps(st_ref, ts, row, value_at, put_y):
    group = RWKV_HEAD // SCAN_ROW_SPLIT
    groups = [slice(i * group, (i + 1) * group) for i in range(SCAN_ROW_SPLIT)]

    def state_dot(name, t, rows):
        acc = st_ref[0, rows, :] * row(name, t, 0)
        for kq in range(1, HALF_HEAD):
            acc = acc + st_ref[kq, rows, :] * row(name, t, kq)
        return acc

    def step(t, sa_halves):
        t_next = jnp.minimum(t + 1, ts - 1)
        out = []
        for rows, sa_half in zip(groups, sa_halves):
            sa = sa_half + pltpu.roll(sa_half, LANES // 2, 1)
            vt = value_at(t, rows)
            y = None
            sa_next = None
            for kq in range(HALF_HEAD):
                s_new = st_ref[kq, rows, :] * row("w", t, kq) + sa * row("b", t, kq) + vt * row("k", t, kq)
                st_ref[kq, rows, :] = s_new
                y_term = s_new * row("r", t, kq)
                a_term = s_new * row("a", t_next, kq)
                y = y_term if y is None else y + y_term
                sa_next = a_term if sa_next is None else sa_next + a_term
            put_y(t, rows, y + pltpu.roll(y, LANES // 2, 1))
            out.append(sa_next)
        return tuple(out)

    lax.fori_loop(0, ts, step, tuple(state_dot("a", 0, rows) for rows in groups))


def _scan_kernel(r_ref, w_ref, k_ref, a_ref, b_ref, v_ref, s0_ref, y_ref, sfin_ref, st_ref, *, ts):
    j = pl.program_id(1)

    @pl.when(j == 0)
    def _():
        st_ref[...] = s0_ref[...]

    refs = dict(r=r_ref, w=w_ref, k=k_ref, a=a_ref, b=b_ref)

    def put_y(t, rows, y):
        y_ref[t, rows, :] = y

    _scan_steps(st_ref, ts, lambda name, t, kq: refs[name][t, pl.ds(kq, 1), :], lambda t, rows: v_ref[t, rows, :], put_y)

    @pl.when(j == pl.num_programs(1) - 1)
    def _():
        sfin_ref[...] = st_ref[...]


def _scan_kernel_cm(r_ref, w_ref, k_ref, a_ref, b_ref, v_ref, s0_ref, y_ref, sfin_ref,
                    st_ref, r_s, w_s, k_s, a_s, b_s, v_s, y_s, *, ts):
    j = pl.program_id(1)

    @pl.when(j == 0)
    def _():
        st_ref[...] = s0_ref[...]

    def head_rows(ref, c0):
        rows = pl.ds(pl.multiple_of(c0 * RWKV_HEADS, RWKV_HEADS), RWKV_HEADS)
        return jnp.concatenate([ref[b, rows, :] for b in range(SCAN_BATCH)], axis=0)

    for src, dst in ((r_ref, r_s), (w_ref, w_s), (k_ref, k_s), (a_ref, a_s), (b_ref, b_s)):
        def key_tile(kq, carry, src=src, dst=dst):
            tile = jnp.concatenate([head_rows(src, kq), head_rows(src, HALF_HEAD + kq)], axis=0)
            dst[kq, :ts, :] = tile.T
            return carry

        lax.fori_loop(0, HALF_HEAD, key_tile, 0, unroll=RELAYOUT_UNROLL)

    value_rows = lambda v: pl.ds(v, ts, stride=RWKV_HEAD)

    def value_tile(v, carry):
        half = head_rows(v_ref, v)
        v_s[value_rows(v), :] = jnp.concatenate([half, half], axis=0).T
        return carry

    lax.fori_loop(0, RWKV_HEAD, value_tile, 0, unroll=RELAYOUT_UNROLL)

    scr = dict(r=r_s, w=w_s, k=k_s, a=a_s, b=b_s)
    step_rows = lambda t, rows: pl.ds(pl.multiple_of(t * RWKV_HEAD, RWKV_HEAD) + rows.start, rows.stop - rows.start)

    def put_y(t, rows, y):
        y_s[step_rows(t, rows), :] = y

    _scan_steps(st_ref, ts, lambda name, t, kq: scr[name][kq, pl.ds(t, 1), :],
                lambda t, rows: v_s[step_rows(t, rows), :], put_y)

    def y_tile(v, carry):
        tile = y_s[value_rows(v), :].T
        rows = pl.ds(pl.multiple_of(v * RWKV_HEADS, RWKV_HEADS), RWKV_HEADS)
        for b in range(SCAN_BATCH):
            y_ref[b, rows, :] = tile[b * RWKV_HEADS:(b + 1) * RWKV_HEADS, :]
        return carry

    lax.fori_loop(0, RWKV_HEAD, y_tile, 0, unroll=RELAYOUT_UNROLL)

    @pl.when(j == pl.num_programs(1) - 1)
    def _():
        sfin_ref[...] = st_ref[...]


def _wkv_scan_cm(r, w, k, a, b, v, s0):
    batch, _, seq = r.shape
    ts = LANES
    assert seq % ts == 0 and batch % SCAN_BATCH == 0
    cm_spec = pl.BlockSpec((SCAN_BATCH, RWKV_DIM, ts), lambda g, j: (g, 0, j))
    state_spec = pl.BlockSpec((None, HALF_HEAD, RWKV_HEAD, LANES), lambda g, j: (g, 0, 0, 0))
    return pl.pallas_call(
        functools.partial(_scan_kernel_cm, ts=ts),
        out_shape=[jax.ShapeDtypeStruct(r.shape, F32), jax.ShapeDtypeStruct(s0.shape, F32)],
        grid=(batch // SCAN_BATCH, seq // ts),
        in_specs=[cm_spec] * 6 + [state_spec],
        out_specs=[cm_spec, state_spec],
        scratch_shapes=[pltpu.VMEM((HALF_HEAD, RWKV_HEAD, LANES), F32)]
                       + [pltpu.VMEM((HALF_HEAD, ts + SCRATCH_ROW_PAD, LANES), F32)] * 5
                       + [pltpu.VMEM((RWKV_HEAD * ts, LANES), F32)] * 2,
        compiler_params=_cparams("parallel", "arbitrary"),
        name="rwkv7_state_scan",
    )(r, w, k, a, b, v, s0)


def _wkv_scan(r, w, k, a, b, v, s0):
    ng, seq = r.shape[:2]
    ts = _row_tile(seq, 128) if seq % 8 == 0 else seq
    half_spec = pl.BlockSpec((None, ts, HALF_HEAD, LANES), lambda g, j: (g, j, 0, 0))
    full_spec = pl.BlockSpec((None, ts, RWKV_HEAD, LANES), lambda g, j: (g, j, 0, 0))
    state_spec = pl.BlockSpec((None, HALF_HEAD, RWKV_HEAD, LANES), lambda g, j: (g, 0, 0, 0))
    return pl.pallas_call(
        functools.partial(_scan_kernel, ts=ts),
        out_shape=[jax.ShapeDtypeStruct(v.shape, F32), jax.ShapeDtypeStruct(s0.shape, F32)],
        grid=(ng, seq // ts),
        in_specs=[half_spec] * 5 + [full_spec, state_spec],
        out_specs=[full_spec, state_spec],
        scratch_shapes=[pltpu.VMEM((HALF_HEAD, RWKV_HEAD, LANES), F32)],
        compiler_params=_cparams("parallel", "arbitrary"),
        name="rwkv7_state_scan",
    )(r, w, k, a, b, v, s0)


def _to_scan_layout(x, batch, seq, dup):
    ng = batch // SCAN_BATCH
    if dup:
        x = x.reshape(ng, SCAN_BATCH, seq, RWKV_HEAD, RWKV_HEADS).transpose(0, 2, 3, 1, 4)
        x = x.reshape(ng, seq, RWKV_HEAD, SCAN_BATCH * RWKV_HEADS)
        return jnp.concatenate([x, x], axis=-1)
    x = x.reshape(ng, SCAN_BATCH, seq, 2, HALF_HEAD, RWKV_HEADS).transpose(0, 2, 4, 3, 1, 5)
    return x.reshape(ng, seq, HALF_HEAD, LANES)


def _from_scan_layout(y, batch, seq):
    ng = batch // SCAN_BATCH
    y = y[..., :LANES // 2].reshape(ng, seq, RWKV_HEAD, SCAN_BATCH, RWKV_HEADS).transpose(0, 3, 1, 2, 4)
    return y.reshape(batch * seq, RWKV_DIM)


def _state_to_scan_layout(s, batch):
    ng = batch // SCAN_BATCH
    s = s.reshape(ng, SCAN_BATCH, RWKV_HEADS, RWKV_HEAD, 2, HALF_HEAD).transpose(0, 5, 3, 4, 1, 2)
    return s.reshape(ng, HALF_HEAD, RWKV_HEAD, LANES)


def _state_from_scan_layout(s, batch):
    ng = batch // SCAN_BATCH
    s = s.reshape(ng, HALF_HEAD, RWKV_HEAD, 2, SCAN_BATCH, RWKV_HEADS).transpose(0, 4, 5, 2, 3, 1)
    return s.reshape(batch, RWKV_HEADS, RWKV_HEAD, RWKV_HEAD)


def _merge_kernel(h_ref, y_ref, bonus_ref, g_ref, oa_ref, gates_ref, lnw_ref, lnb_ref, ones_ref, wob_ref, woa_ref,
                  wout_ref, post_ref, o_ref, *, y_channel_major):
    ones = ones_ref[...]
    y = y_ref[...].T if y_channel_major else y_ref[...]
    inv_n = 1.0 / RWKV_HEAD
    mu = _segsum(y, ones) * inv_n
    yc = y - mu
    var = _segsum(yc * yc, ones) * inv_n
    yn = yc * lax.rsqrt(var + GN_EPS) * lnw_ref[...] + lnb_ref[...] + bonus_ref[...]
    o_b = _dot((yn * g_ref[...]).astype(BF16), wob_ref[...])
    o_a = _dot(oa_ref[...].astype(BF16), woa_ref[...])
    d = o_ref.shape[1]
    gates = gates_ref[...]
    mix = _sigmoid(gates[:, :d]) * o_a + _sigmoid(gates[:, d:]) * o_b
    o_ref[...] = h_ref[...] + _rms(_dot(mix.astype(BF16), wout_ref[...]), post_ref[...])


def _merge(h, y, bonus, g, oa, gates, lnw, lnb, ones, wob, woa, wout, post_g):
    t, d = h.shape
    tm = _row_tile(t, 512)
    consts = (lnw, lnb, ones, wob, woa, wout, post_g)
    y_spec = _rows_spec(tm, RWKV_DIM)
    if y.ndim == 3:
        tiles_per_seq = y.shape[2] // tm
        y_spec = pl.BlockSpec((None, RWKV_DIM, tm), lambda i: (i // tiles_per_seq, 0, i % tiles_per_seq))
    return pl.pallas_call(
        functools.partial(_merge_kernel, y_channel_major=y.ndim == 3),
        out_shape=jax.ShapeDtypeStruct((t, d), F32),
        grid=(t // tm,),
        in_specs=[_rows_spec(tm, d), y_spec] + [_rows_spec(tm, RWKV_DIM)] * 3 + [_rows_spec(tm, 2 * d)]
                 + [_const_spec(c.shape) for c in consts],
        out_specs=_rows_spec(tm, d),
        compiler_params=_cparams("parallel"),
        name="gated_merge_out_proj",
    )(h, y, bonus, g, oa, gates, *consts)


def _ple_kernel(h_ref, p_ref, pre_ref, wgate_ref, wproj_ref, post_ref, o_ref):
    h = h_ref[...]
    gp = _sigmoid(_dot(_rms(h, pre_ref[...]).astype(BF16), wgate_ref[...]))
    emb = _dot(p_ref[...].astype(BF16), wproj_ref[...])
    o_ref[...] = h + _rms(gp * emb, post_ref[...])


def _ple(h, p, pre_g, wgate, wproj, post_g):
    t, d = h.shape
    tm = _row_tile(t, 512)
    consts = (pre_g, wgate, wproj, post_g)
    return pl.pallas_call(
        _ple_kernel,
        out_shape=jax.ShapeDtypeStruct((t, d), F32),
        grid=(t // tm,),
        in_specs=[_rows_spec(tm, d), _rows_spec(tm, p.shape[1])] + [_const_spec(c.shape) for c in consts],
        out_specs=_rows_spec(tm, d),
        compiler_params=_cparams("parallel"),
        name="per_layer_embedding",
    )(h, p, *consts)


def _rope_tables(pos):
    inv = ROPE_THETA ** (-jnp.arange(0, ROPE_DIM, 2, dtype=F32) / ROPE_DIM)
    ang = pos[:, None] * inv[None, :]
    cos, sin = jnp.cos(ang), jnp.sin(ang)
    reps = ROPE_PAD // ROPE_DIM
    return jnp.tile(jnp.concatenate([cos, cos], axis=-1), (1, reps)), jnp.tile(jnp.concatenate([-sin, sin], axis=-1), (1, reps))


def _row(v):
    return v.reshape(1, -1).astype(F32)


def kernel(x_prompt, x_sample, p_prompt, p_sample, cache_ckv, cache_kpe, state_wkv, state_shift, page_table, ffn1_pre_g, ffn1_wg, ffn1_wu, ffn1_wd, ffn1_post_g, mix_pre_g, w_in, q_norm_g, w_uq, kv_norm_g, w_uk, w_uv, w_oa, rwkv_mu, rwkv_w0, rwkv_w_up, rwkv_a0, rwkv_a_up, rwkv_g_up, rwkv_k_k, rwkv_k_a, rwkv_r_k, rwkv_ln_w, rwkv_ln_b, w_ob, w_out, mix_post_g, ffn2_pre_g, ffn2_wg, ffn2_wu, ffn2_wd, ffn2_post_g, ple_pre_g, w_ple_gate, w_ple_proj, ple_post_g):
    depth = w_in.shape[0]
    assert depth == 1, "single-layer step"
    batch, seq, d_model = x_prompt.shape
    dec_batch, dec_seq, _ = x_sample.shape
    n_pages = page_table.shape[1]
    page = cache_ckv.shape[2]
    kv_rank = cache_ckv.shape[3]
    q_rank = w_uq.shape[1]
    shift_dim = state_shift.shape[2]
    tp, ts_tok = batch * seq, dec_batch * dec_seq
    assert batch % SCAN_BATCH == 0 and dec_batch % SCAN_BATCH == 0
    assert shift_dim == 3 * RWKV_DIM + W_LORA + A_LORA + G_LORA
    bf = lambda w: w.astype(BF16)

    w_in0 = w_in[0]
    c0, c1, c2, c3 = q_rank, q_rank + kv_rank, q_rank + kv_rank + ROPE_DIM, q_rank + kv_rank + ROPE_DIM + shift_dim
    wq, wkv, wrw, wgate = bf(w_in0[:, :c0]), bf(w_in0[:, c0:c1]), bf(w_in0[:, c2:c3]), bf(w_in0[:, c3:])
    wkpe = bf(jnp.pad(w_in0[:, c1:c2], ((0, 0), (0, ROPE_PAD - ROPE_DIM))))
    wuq3 = w_uq[0].reshape(q_rank, MLA_HEADS, NOPE_DIM + ROPE_DIM)
    wuq_nope = wuq3[:, :, :NOPE_DIM].reshape(q_rank, MLA_HEADS * NOPE_DIM)
    wuq_pe = jnp.pad(wuq3[:, :, NOPE_DIM:], ((0, 0), (0, 0), (0, ROPE_PAD - ROPE_DIM))).reshape(q_rank, MLA_HEADS * ROPE_PAD)
    wuq = bf(jnp.concatenate([wuq_nope, wuq_pe], axis=1))
    wukt = bf(w_uk[0].transpose(1, 2, 0))
    wuv = bf(w_uv[0].transpose(1, 0, 2))
    wuvt = bf(w_uv[0].transpose(1, 2, 0))
    perm = (jnp.arange(RWKV_DIM) % RWKV_HEADS) * RWKV_HEAD + jnp.arange(RWKV_DIM) // RWKV_HEADS
    perm_rw = jnp.concatenate([perm, RWKV_DIM + perm, 2 * RWKV_DIM + perm, jnp.arange(3 * RWKV_DIM, shift_dim)])
    wrw = wrw[:, perm_rw]
    wa_up = jnp.zeros((W_LORA + A_LORA, 2 * RWKV_DIM), F32)
    wa_up = bf(wa_up.at[:W_LORA, :RWKV_DIM].set(rwkv_w_up[0][:, perm]).at[W_LORA:, RWKV_DIM:].set(rwkv_a_up[0][:, perm]))
    seg = perm // RWKV_HEAD
    ones = (seg[:, None] == seg[None, :]).astype(BF16)
    chan = lambda v: _row(v)[:, perm]
    ffn1 = (_row(ffn1_pre_g), bf(ffn1_wg[0]), bf(ffn1_wu[0]), bf(ffn1_wd[0]), _row(ffn1_post_g))
    ffn2 = (_row(ffn2_pre_g), bf(ffn2_wg[0]), bf(ffn2_wu[0]), bf(ffn2_wd[0]), _row(ffn2_post_g))
    inproj_w = (_row(mix_pre_g), wq, wkv, wkpe, wrw, wgate, _row(q_norm_g), wuq, _row(kv_norm_g), wukt)
    prep_w = (_row(rwkv_mu)[:, perm_rw], chan(rwkv_w0), chan(rwkv_a0), wa_up, bf(rwkv_g_up[0][:, perm]), chan(rwkv_k_k),
              chan(rwkv_k_a), chan(rwkv_r_k), ones)
    merge_w = (chan(rwkv_ln_w), chan(rwkv_ln_b), ones, bf(w_ob[0][perm, :]), bf(w_oa[0]), bf(w_out[0]), _row(mix_post_g))
    last_row = lambda rw, nb, ns: rw.reshape(nb, ns, shift_dim)[:, -1][:, jnp.argsort(perm_rw)]
    ple_w = (_row(ple_pre_g), bf(w_ple_gate[0]), bf(w_ple_proj[0]), _row(ple_post_g))

    def rwkv_mix(rw, shift0, s0, nb, ns):
        r, w, k, v, a, b, g, bonus = _rwkv_prep(rw, shift0, ns, *prep_w)
        if r.ndim == 3:
            y, s_fin = _wkv_scan_cm(r, w, k, a, b, v, _state_to_scan_layout(s0, nb))
            return y, bonus, g, _state_from_scan_layout(s_fin, nb)
        halves = [_to_scan_layout(t_, nb, ns, False) for t_ in (r, w, k, a, b)]
        y, s_fin = _wkv_scan(*halves, _to_scan_layout(v, nb, ns, True), _state_to_scan_layout(s0, nb))
        return _from_scan_layout(y, nb, ns), bonus, g, _state_from_scan_layout(s_fin, nb)

    tq = _row_tile(seq, 256)
    cos_p, sin_p = _rope_tables(jnp.arange(seq, dtype=F32))
    h1 = _ffn(x_prompt.reshape(tp, d_model), *ffn1)
    qcat, ckv_p, kpe_p, rw_p, gates, kcat, ckvt = _inproj(h1, *inproj_w, cos_p, sin_p, tq)
    oa = _attn_prompt(qcat, kcat, ckvt, wuvt, batch, seq, tq)
    y, bonus, g, wkv_p = rwkv_mix(rw_p, jnp.zeros((batch, shift_dim), F32),
                                  jnp.zeros((batch, RWKV_HEADS, RWKV_HEAD, RWKV_HEAD), F32), batch, seq)
    h2 = _merge(h1, y, bonus, g, oa, gates, *merge_w)
    h3 = _ffn(h2, *ffn2)
    y_prompt = _ple(h3, p_prompt[0].reshape(tp, -1), *ple_w)

    pos_s = jnp.tile(n_pages * page + jnp.arange(dec_seq, dtype=F32), dec_batch)
    cos_s, sin_s = _rope_tables(pos_s)
    h1 = _ffn(x_sample.reshape(ts_tok, d_model), *ffn1)
    qcat, ckv_s, kpe_s, rw_s, gates = _inproj(h1, *inproj_w, cos_s, sin_s, 0)
    qw = kv_rank + ROPE_PAD
    q3 = qcat.reshape(dec_batch, dec_seq, MLA_HEADS, qw).transpose(0, 2, 1, 3).reshape(
        dec_batch, MLA_HEADS * dec_seq, qw).astype(F32)
    o3 = _attn_paged(page_table, q3, ckv_s.reshape(dec_batch, dec_seq, kv_rank),
                     kpe_s.reshape(dec_batch, dec_seq, ROPE_DIM), cache_ckv, jnp.swapaxes(cache_kpe, 2, 3))
    o_lat = o3.reshape(dec_batch, MLA_HEADS, dec_seq, kv_rank).transpose(0, 2, 1, 3).reshape(ts_tok, MLA_HEADS * kv_rank)
    oa = _uv_proj(o_lat, wuv)
    y, bonus, g, wkv_s = rwkv_mix(rw_s, state_shift[0][:, perm_rw], state_wkv[0].astype(F32), dec_batch, dec_seq)
    h2 = _merge(h1, y, bonus, g, oa, gates, *merge_w)
    h3 = _ffn(h2, *ffn2)
    y_sample = _ple(h3, p_sample[0].reshape(ts_tok, -1), *ple_w)

    return (y_prompt.reshape(batch, seq, d_model), y_sample.reshape(dec_batch, dec_seq, d_model),
            ckv_p.reshape(1, batch, seq, kv_rank), kpe_p.reshape(1, batch, seq, ROPE_DIM),
            wkv_p[None], last_row(rw_p, batch, seq)[None],
            ckv_s.reshape(1, dec_batch, dec_seq, kv_rank), kpe_s.reshape(1, dec_batch, dec_seq, ROPE_DIM),
            wkv_s[None], last_row(rw_s, dec_batch, dec_seq)[None])
```

```python
import functools
import math

import jax
import jax.numpy as jnp
from jax import lax
from jax.experimental import pallas as pl
from jax.experimental.pallas import tpu as pltpu

F32 = jnp.float32
BF16 = jnp.bfloat16

MLA_HEADS = 8
NOPE_DIM = 64
ROPE_DIM = 32
V_DIM = 64
ROPE_THETA = 10000.0
SOFTMAX_SCALE = (NOPE_DIM + ROPE_DIM) ** -0.5
Q_SCALE = SOFTMAX_SCALE * math.log2(math.e)
RWKV_HEADS = 8
RWKV_HEAD = 64
RWKV_DIM = RWKV_HEADS * RWKV_HEAD
W_LORA = 64
A_LORA = 64
G_LORA = 128
GN_EPS = 64e-5
NORM_EPS = 1e-6

LANES = 128
VMEM_LIMIT_BYTES = 56 * 1024 * 1024

ROPE_PAD = LANES
SCAN_BATCH = 8
HALF_HEAD = RWKV_HEAD // 2
SCAN_ROW_SPLIT = 2
RELAYOUT_UNROLL = 8
SCRATCH_ROW_PAD = 8
MAX_PAGES_PER_STREAM = 16
PAGED_STREAMS = 2
PAGED_SLOTS = 3
SCORE_LOOKAHEAD = 8


def _cparams(*sem):
    return pltpu.CompilerParams(dimension_semantics=sem, vmem_limit_bytes=VMEM_LIMIT_BYTES)


def _dot(a, b):
    return jnp.dot(a, b, preferred_element_type=F32)


def _dot_nt(a, b):
    return lax.dot_general(a, b, (((1,), (1,)), ((), ())), preferred_element_type=F32)


def _rms(x, g):
    return x * lax.rsqrt(jnp.mean(x * x, axis=-1, keepdims=True) + NORM_EPS) * g


def _sigmoid(x):
    return 1.0 / (1.0 + jnp.exp(-x))


def _rope(x, cos, sin_signed):
    lane = lax.broadcasted_iota(jnp.int32, x.shape, 1)
    swapped = jnp.where((lane % ROPE_DIM) < ROPE_DIM // 2,
                        pltpu.roll(x, LANES - ROPE_DIM // 2, 1), pltpu.roll(x, ROPE_DIM // 2, 1))
    return x * cos + swapped * sin_signed


def _segsum(x, ones_bf16):
    hi = x.astype(BF16)
    lo = (x - hi.astype(F32)).astype(BF16)
    return _dot(hi, ones_bf16) + _dot(lo, ones_bf16)


def _row_tile(n, cap):
    t = cap
    while n % t:
        t //= 2
    assert t >= 8, (n, cap)
    return t


def _const_spec(shape):
    nd = len(shape)
    return pl.BlockSpec(shape, lambda *_: (0,) * nd)


def _rows_spec(tm, width):
    return pl.BlockSpec((tm, width), lambda i: (i, 0))


def _ffn_kernel(x_ref, pre_ref, wg_ref, wu_ref, wd_ref, post_ref, o_ref, *, n_chunks):
    x = x_ref[...]
    xn = _rms(x, pre_ref[...]).astype(BF16)
    fc = wg_ref.shape[1] // n_chunks
    acc = None
    for c in range(n_chunks):
        g = _dot(xn, wg_ref[:, c * fc:(c + 1) * fc])
        u = _dot(xn, wu_ref[:, c * fc:(c + 1) * fc])
        a = (g * _sigmoid(g) * u).astype(BF16)
        d = _dot(a, wd_ref[c * fc:(c + 1) * fc, :])
        acc = d if acc is None else acc + d
    o_ref[...] = x + 0.5 * _rms(acc, post_ref[...])


def _ffn(x, pre_g, wg, wu, wd, post_g):
    t, d = x.shape
    f = wg.shape[1]
    tm = _row_tile(t, 512)
    n_chunks = 2 if f % (2 * LANES) == 0 else 1
    return pl.pallas_call(
        functools.partial(_ffn_kernel, n_chunks=n_chunks),
        out_shape=jax.ShapeDtypeStruct((t, d), F32),
        grid=(t // tm,),
        in_specs=[_rows_spec(tm, d), _const_spec((1, d)), _const_spec((d, f)), _const_spec((d, f)),
                  _const_spec((f, d)), _const_spec((1, d))],
        out_specs=_rows_spec(tm, d),
        compiler_params=_cparams("parallel"),
        name="ffn_half_step",
    )(x, pre_g, wg, wu, wd, post_g)


def _inproj_kernel(h_ref, g_ref, wq_ref, wkv_ref, wkpe_ref, wrw_ref, wgate_ref, qng_ref, wuq_ref, kvg_ref,
                   wukt_ref, cos_ref, sin_ref, qcat_ref, ckv_ref, kpe_ref, rw_ref, gates_ref, *kt_refs, tk):
    u = _rms(h_ref[...], g_ref[...]).astype(BF16)
    rw_ref[...] = _dot(u, wrw_ref[...])
    gates_ref[...] = _dot(u, wgate_ref[...])
    ckv = _rms(_dot(u, wkv_ref[...]), kvg_ref[...])
    ckv_ref[...] = ckv
    cos = cos_ref[...]
    sin = sin_ref[...]
    kpe = _rope(_dot(u, wkpe_ref[...]), cos, sin)
    kpe_ref[...] = kpe[:, :ROPE_DIM]
    kv_rank = ckv.shape[1]
    if kt_refs:
        kcat_ref, ckvt_ref = kt_refs
        kcat_ref[:, :kv_rank] = ckv.astype(BF16)
        kcat_ref[:, kv_rank:] = kpe.astype(BF16)
        for j in range(ckv.shape[0] // tk):
            ckvt_ref[j] = ckv[j * tk:(j + 1) * tk, :].T.astype(BF16)
    cq = _rms(_dot(u, wq_ref[...]), qng_ref[...]).astype(BF16)
    q = _dot(cq, wuq_ref[...])
    nope_w = MLA_HEADS * NOPE_DIM
    qw = kv_rank + ROPE_PAD
    for h in range(MLA_HEADS):
        nope = q[:, h * NOPE_DIM:(h + 1) * NOPE_DIM].astype(BF16)
        qcat_ref[:, h * qw:h * qw + kv_rank] = (_dot(nope, wukt_ref[h]) * Q_SCALE).astype(BF16)
        pe = q[:, nope_w + h * ROPE_PAD: nope_w + (h + 1) * ROPE_PAD]
        qcat_ref[:, h * qw + kv_rank:(h + 1) * qw] = (_rope(pe, cos, sin) * Q_SCALE).astype(BF16)


def _inproj(h, g, wq, wkv, wkpe, wrw, wgate, qng, wuq, kvg, wukt, cos, sin, tk):
    t, d = h.shape
    tm = _row_tile(t, 512)
    kv_rank = wkv.shape[1]
    qw = kv_rank + ROPE_PAD
    n_pos = cos.shape[0] // tm
    out_shape = [jax.ShapeDtypeStruct((t, MLA_HEADS * qw), BF16), jax.ShapeDtypeStruct((t, kv_rank), F32),
                 jax.ShapeDtypeStruct((t, ROPE_DIM), F32), jax.ShapeDtypeStruct((t, wrw.shape[1]), F32),
                 jax.ShapeDtypeStruct((t, wgate.shape[1]), F32)]
    out_specs = [_rows_spec(tm, s.shape[1]) for s in out_shape]
    if tk:
        assert tm % tk == 0
        out_shape += [jax.ShapeDtypeStruct((t, qw), BF16), jax.ShapeDtypeStruct((t // tk, kv_rank, tk), BF16)]
        out_specs += [_rows_spec(tm, qw), pl.BlockSpec((tm // tk, kv_rank, tk), lambda i: (i, 0, 0))]
    consts = (g, wq, wkv, wkpe, wrw, wgate, qng, wuq, kvg, wukt)
    pos_spec = pl.BlockSpec((tm, ROPE_PAD), lambda i: (i % n_pos, 0))
    return pl.pallas_call(
        functools.partial(_inproj_kernel, tk=tk),
        out_shape=out_shape,
        grid=(t // tm,),
        in_specs=[_rows_spec(tm, d)] + [_const_spec(c.shape) for c in consts] + [pos_spec, pos_spec],
        out_specs=out_specs,
        compiler_params=_cparams("parallel"),
        name="mixer_in_proj",
    )(h, *consts, cos, sin)


def _attn_kernel(qcat_ref, kcat_ref, ckvt_ref, wuvt_ref, o_ref, m_scr, l_scr, acc_scr, *, tq):
    i = pl.program_id(1)
    qw = kcat_ref.shape[1]
    m_scr[...] = jnp.full(m_scr.shape, -jnp.inf, F32)
    l_scr[...] = jnp.zeros(l_scr.shape, F32)
    acc_scr[...] = jnp.zeros(acc_scr.shape, F32)

    def block(kb, diagonal):
        off = pl.multiple_of(kb * tq, tq)
        kcat = kcat_ref[pl.ds(off, tq), :]
        vt = ckvt_ref[kb]
        def scores(h):
            return _dot_nt(kcat, qcat_ref[:, h * qw:(h + 1) * qw])

        pending = [scores(h) for h in range(min(SCORE_LOOKAHEAD, MLA_HEADS))]
        for h in range(MLA_HEADS):
            if h + SCORE_LOOKAHEAD < MLA_HEADS:
                pending.append(scores(h + SCORE_LOOKAHEAD))
            s = pending.pop(0)
            if diagonal:
                key = lax.broadcasted_iota(jnp.int32, s.shape, 0)
                qry = lax.broadcasted_iota(jnp.int32, s.shape, 1)
                s = jnp.where(key <= qry, s, -jnp.inf)
            m_prev = m_scr[h]
            m_new = jnp.maximum(m_prev, jnp.max(s, axis=0, keepdims=True))
            alpha = jnp.exp2(m_prev - m_new)
            p = jnp.exp2(s - m_new)
            l_scr[h] = alpha * l_scr[h] + jnp.sum(p, axis=0, keepdims=True)
            acc_scr[h] = alpha * acc_scr[h] + _dot(vt, p.astype(BF16))
            m_scr[h] = m_new

    def off_diagonal(kb, carry):
        block(kb, False)
        return carry

    lax.fori_loop(0, i, off_diagonal, 0)
    block(i, True)
    outs = [_dot(wuvt_ref[h], (acc_scr[h] / l_scr[h]).astype(BF16)) for h in range(MLA_HEADS)]
    o_ref[...] = jnp.concatenate(outs, axis=0).T


def _attn_prompt(qcat, kcat, ckvt, wuvt, batch, seq, tq):
    qw = kcat.shape[1]
    kv_rank = ckvt.shape[1]
    nq = seq // tq
    return pl.pallas_call(
        functools.partial(_attn_kernel, tq=tq),
        out_shape=jax.ShapeDtypeStruct((batch * seq, MLA_HEADS * V_DIM), F32),
        grid=(batch, nq),
        in_specs=[pl.BlockSpec((tq, MLA_HEADS * qw), lambda b, i: (b * nq + i, 0)),
                  pl.BlockSpec((seq, qw), lambda b, i: (b, 0)),
                  pl.BlockSpec((nq, kv_rank, tq), lambda b, i: (b, 0, 0)),
                  _const_spec(wuvt.shape)],
        out_specs=pl.BlockSpec((tq, MLA_HEADS * V_DIM), lambda b, i: (b * nq + i, 0)),
        scratch_shapes=[pltpu.VMEM((MLA_HEADS, 1, tq), F32), pltpu.VMEM((MLA_HEADS, 1, tq), F32),
                        pltpu.VMEM((MLA_HEADS, kv_rank, tq), F32)],
        compiler_params=_cparams("parallel", "arbitrary"),
        name="mla_prompt_attention",
    )(qcat, kcat, ckvt, wuvt)


def _paged_kernel(pt_ref, q_ref, knew_ref, kpnew_ref, cache_ckv, cache_kpe_t, o_ref,
                  kc_buf, kp_buf, sem, *, n_batch, chunks_per_batch, pages_per_stream, page, dec_seq):
    total = n_batch * chunks_per_batch
    pages_per_chunk = PAGED_STREAMS * pages_per_stream
    kv_rank = kc_buf.shape[2]

    def page_copies(g, slot, p):
        pg = pt_ref[g * pages_per_chunk + p]
        rows = pl.ds(p * page, page)
        return (pltpu.make_async_copy(cache_ckv.at[0, pg], kc_buf.at[slot, rows, :], sem.at[0, slot]),
                pltpu.make_async_copy(cache_kpe_t.at[0, pg], kp_buf.at[slot, :, rows], sem.at[1, slot]))

    def start_chunk(g, slot):
        for p in range(pages_per_chunk):
            for cp in page_copies(g, slot, p):
                cp.start()

    def wait_chunk(g, slot):
        for p in range(pages_per_chunk):
            for cp in page_copies(g, slot, p):
                cp.wait()

    for g0 in range(min(PAGED_SLOTS - 1, total)):
        start_chunk(g0, g0)

    def batch_body(b, carry):
        qf = q_ref[b]
        qlf = qf[:, :kv_rank]
        qpf = qf[:, kv_rank:kv_rank + ROPE_DIM]
        qb = qlf.astype(BF16)
        qpb = qpf.astype(BF16)
        rows = qf.shape[0]
        stream_rows = pages_per_stream * page

        def chunk_body(c, states):
            g = b * chunks_per_batch + c
            slot = g % PAGED_SLOTS
            ahead = g + PAGED_SLOTS - 1

            @pl.when(ahead < total)
            def _():
                start_chunk(ahead, ahead % PAGED_SLOTS)

            wait_chunk(g, slot)
            keys, scores = [], []
            for st in range(PAGED_STREAMS):
                kc = kc_buf[slot, st * stream_rows:(st + 1) * stream_rows, :].astype(BF16)
                kp_t = kp_buf[slot, :, st * stream_rows:(st + 1) * stream_rows].astype(BF16)
                keys.append(kc)
                scores.append(_dot_nt(qb, kc) + _dot(qpb, kp_t))
            new_states = []
            for (m_prev, l_prev, acc), kc, s in zip(states, keys, scores):
                m_new = jnp.maximum(m_prev, jnp.max(s, axis=-1, keepdims=True))
                alpha = jnp.exp2(m_prev - m_new)
                p = jnp.exp2(s - m_new)
                l_new = alpha * l_prev + jnp.sum(p, axis=-1, keepdims=True)
                new_states.append((m_new, l_new, alpha * acc + _dot(p.astype(BF16), kc)))
            return tuple(new_states)

        init = tuple((jnp.full((rows, 1), -jnp.inf, F32), jnp.zeros((rows, 1), F32), jnp.zeros(qlf.shape, F32))
                     for _ in range(PAGED_STREAMS))
        states = lax.fori_loop(0, chunks_per_batch, chunk_body, init)

        knew = knew_ref[b]
        kpnew = kpnew_ref[b]
        tok = lax.broadcasted_iota(jnp.int32, (rows, 1), 0) % dec_seq
        s_new = []
        for j in range(dec_seq):
            sj = (jnp.sum(qlf * knew[j:j + 1, :], axis=-1, keepdims=True)
                  + jnp.sum(qpf * kpnew[j:j + 1, :], axis=-1, keepdims=True))
            s_new.append(jnp.where(tok >= j, sj, -jnp.inf))
        m_new = states[0][0]
        for m_st, _, _ in states[1:]:
            m_new = jnp.maximum(m_new, m_st)
        for sj in s_new:
            m_new = jnp.maximum(m_new, sj)
        l_new = jnp.zeros((rows, 1), F32)
        acc = jnp.zeros(qlf.shape, F32)
        for m_st, l_st, acc_st in states:
            alpha = jnp.exp2(m_st - m_new)
            l_new = l_new + alpha * l_st
            acc = acc + alpha * acc_st
        for j, sj in enumerate(s_new):
            pj = jnp.exp2(sj - m_new)
            l_new = l_new + pj
            acc = acc + pj * knew[j:j + 1, :]
        o_ref[b] = acc / l_new
        return carry

    lax.fori_loop(0, n_batch, batch_body, 0)


def _attn_paged(page_table, q3, knew3, kpnew3, cache_ckv, cache_kpe_t):
    n_batch, n_pages = page_table.shape
    page = cache_ckv.shape[2]
    kv_rank = cache_ckv.shape[3]
    dec_seq = knew3.shape[1]
    pps = MAX_PAGES_PER_STREAM
    while n_pages % (PAGED_STREAMS * pps):
        pps //= 2
    assert pps >= 1, n_pages
    ppc = PAGED_STREAMS * pps
    kernel = functools.partial(_paged_kernel, n_batch=n_batch, chunks_per_batch=n_pages // ppc,
                               pages_per_stream=pps, page=page, dec_seq=dec_seq)
    vmem = lambda shape: pl.BlockSpec(shape, lambda i, pt: (0,) * len(shape))
    any_spec = pl.BlockSpec(memory_space=pl.ANY)
    out_shape = (n_batch, q3.shape[1], kv_rank)
    return pl.pallas_call(
        kernel,
        out_shape=jax.ShapeDtypeStruct(out_shape, F32),
        grid_spec=pltpu.PrefetchScalarGridSpec(
            num_scalar_prefetch=1,
            grid=(1,),
            in_specs=[vmem(q3.shape), vmem(knew3.shape), vmem(kpnew3.shape), any_spec, any_spec],
            out_specs=vmem(out_shape),
            scratch_shapes=[pltpu.VMEM((PAGED_SLOTS, ppc * page, kv_rank), F32),
                            pltpu.VMEM((PAGED_SLOTS, cache_kpe_t.shape[2], ppc * page), F32),
                            pltpu.SemaphoreType.DMA((2, PAGED_SLOTS))]),
        compiler_params=_cparams("arbitrary"),
        name="mla_paged_attention",
    )(page_table.reshape(-1), q3, knew3, kpnew3, cache_ckv, cache_kpe_t)


def _uv_kernel(o_ref, wuv_ref, out_ref):
    kv_rank = wuv_ref.shape[1]
    outs = [_dot(o_ref[:, h * kv_rank:(h + 1) * kv_rank].astype(BF16), wuv_ref[h]) for h in range(MLA_HEADS)]
    out_ref[...] = jnp.concatenate(outs, axis=-1)


def _uv_proj(o_lat, wuv):
    t = o_lat.shape[0]
    return pl.pallas_call(
        _uv_kernel,
        out_shape=jax.ShapeDtypeStruct((t, MLA_HEADS * V_DIM), F32),
        grid=(1,),
        in_specs=[_const_spec(o_lat.shape), _const_spec(wuv.shape)],
        out_specs=_const_spec((t, MLA_HEADS * V_DIM)),
        compiler_params=_cparams("arbitrary"),
        name="mla_value_up_proj",
    )(o_lat, wuv)


def _softplus(z):
    return jnp.maximum(z, 0.0) + jnp.log(1.0 + jnp.exp(-jnp.abs(z)))


def _rwkv_prep_kernel(rw_ref, first_ref, *refs, seq, tiles_per_seq):
    x = rw_ref[...]
    row = lax.broadcasted_iota(jnp.int32, x.shape, 0)
    rolled = pltpu.roll(x, 1, 0)
    if tiles_per_seq:
        prev8_ref, refs = refs[0], refs[1:]
        at_start = pl.program_id(0) % tiles_per_seq == 0
        first = jnp.where(at_start, first_ref[...], prev8_ref[7:8, :])
        prev = jnp.where(row == 0, first, rolled)
    else:
        prev = jnp.where(row % seq == 0, first_ref[...], rolled)
    (mu_ref, w0_ref, a0_ref, wa_up_ref, g_up_ref, kk_ref, ka_ref, rk_ref, ones_ref,
     r_o, w_o, k_o, v_o, a_o, b_o, g_o, bonus_o) = refs
    xr = x + (prev - x) * mu_ref[...]
    d = RWKV_DIM
    r = xr[:, 0:d]
    k = xr[:, d:2 * d]
    v = xr[:, 2 * d:3 * d]
    xwa = xr[:, 3 * d:3 * d + W_LORA + A_LORA]
    xg = xr[:, 3 * d + W_LORA + A_LORA:]
    lane = lax.broadcasted_iota(jnp.int32, xwa.shape, 1)
    wa = _dot(jnp.where(lane < W_LORA, jnp.tanh(xwa), xwa).astype(BF16), wa_up_ref[...])
    w = -_softplus(-(w0_ref[...] + wa[:, :d])) - 0.5
    a = _sigmoid(a0_ref[...] + wa[:, d:])
    ones = ones_ref[...]
    kk = k * kk_ref[...]
    kk = kk / jnp.maximum(jnp.sqrt(_segsum(kk * kk, ones)), 1e-12)
    k2 = k * (1.0 + (a - 1.0) * ka_ref[...])
    scan_ops = ((r_o, r), (w_o, jnp.exp(-jnp.exp(w))), (k_o, k2), (v_o, v), (a_o, -kk), (b_o, kk * a))
    for o_ref, val in scan_ops:
        o_ref[...] = val.T if tiles_per_seq else val
    g_o[...] = _dot(_sigmoid(xg).astype(BF16), g_up_ref[...])
    bonus_o[...] = _segsum(r * k2 * rk_ref[...], ones) * v


def _rwkv_prep(rw, shift0, seq, mu, w0, a0, wa_up, g_up, k_k, k_a, r_k, ones):
    t, width = rw.shape
    tm = _row_tile(t, 512)
    consts = (mu, w0, a0, wa_up, g_up, k_k, k_a, r_k, ones)
    if seq % tm == 0:
        tiles_per_seq = seq // tm
        first = shift0.reshape(-1, 1, width)
        lead = [first, rw]
        lead_specs = [pl.BlockSpec((None, 1, width), lambda i: (i // tiles_per_seq, 0, 0)),
                      pl.BlockSpec((8, width), lambda i: (jnp.maximum(i * (tm // 8) - 1, 0), 0))]
    else:
        assert tm % seq == 0
        tiles_per_seq = 0
        lead = [jnp.repeat(shift0, seq, axis=0)]
        lead_specs = [_rows_spec(tm, width)]
    out_shape = [jax.ShapeDtypeStruct((t, RWKV_DIM), F32)] * 8
    out_specs = [_rows_spec(tm, RWKV_DIM)] * 8
    if tiles_per_seq:
        out_shape[:6] = [jax.ShapeDtypeStruct((t // seq, RWKV_DIM, seq), F32)] * 6
        out_specs[:6] = [pl.BlockSpec((None, RWKV_DIM, tm), lambda i: (i // tiles_per_seq, 0, i % tiles_per_seq))] * 6
    return pl.pallas_call(
        functools.partial(_rwkv_prep_kernel, seq=seq, tiles_per_seq=tiles_per_seq),
        out_shape=out_shape,
        grid=(t // tm,),
        in_specs=[_rows_spec(tm, width)] + lead_specs + [_const_spec(c.shape) for c in consts],
        out_specs=out_specs,
        compiler_params=_cparams("parallel"),
        name="rwkv_prep",
    )(rw, *lead, *consts)


def _scan_steps(st_ref, ts, row, value_at, put_y):
    group = RWKV_HEAD // SCAN_ROW_SPLIT
    groups = [slice(i * group, (i + 1) * group) for i in range(SCAN_ROW_SPLIT)]

    def state_dot(name, t, rows):
        acc = st_ref[0, rows, :] * row(name, t, 0)
        for kq in range(1, HALF_HEAD):
            acc = acc + st_ref[kq, rows, :] * row(name, t, kq)
        return acc

    def step(t, sa_halves):
        t_next = jnp.minimum(t + 1, ts - 1)
        out = []
        for rows, sa_half in zip(groups, sa_halves):
            sa = sa_half + pltpu.roll(sa_half, LANES // 2, 1)
            vt = value_at(t, rows)
            y = None
            sa_next = None
            for kq in range(HALF_HEAD):
                s_new = st_ref[kq, rows, :] * row("w", t, kq) + sa * row("b", t, kq) + vt * row("k", t, kq)
                st_ref[kq, rows, :] = s_new
                y_term = s_new * row("r", t, kq)
                a_term = s_new * row("a", t_next, kq)
                y = y_term if y is None else y + y_term
                sa_next = a_term if sa_next is None else sa_next + a_term
            put_y(t, rows, y + pltpu.roll(y, LANES // 2, 1))
            out.append(sa_next)
        return tuple(out)

    lax.fori_loop(0, ts, step, tuple(state_dot("a", 0, rows) for rows in groups))


def _scan_kernel(r_ref, w_ref, k_ref, a_ref, b_ref, v_ref, s0_ref, y_ref, sfin_ref, st_ref, *, ts):
    j = pl.program_id(1)

    @pl.when(j == 0)
    def _():
        st_ref[...] = s0_ref[...]

    refs = dict(r=r_ref, w=w_ref, k=k_ref, a=a_ref, b=b_ref)

    def put_y(t, rows, y):
        y_ref[t, rows, :] = y

    _scan_steps(st_ref, ts, lambda name, t, kq: refs[name][t, pl.ds(kq, 1), :], lambda t, rows: v_ref[t, rows, :], put_y)

    @pl.when(j == pl.num_programs(1) - 1)
    def _():
        sfin_ref[...] = st_ref[...]


def _scan_kernel_cm(r_ref, w_ref, k_ref, a_ref, b_ref, v_ref, s0_ref, y_ref, sfin_ref,
                    st_ref, r_s, w_s, k_s, a_s, b_s, v_s, y_s, *, ts):
    j = pl.program_id(1)

    @pl.when(j == 0)
    def _():
        st_ref[...] = s0_ref[...]

    def head_rows(ref, c0):
        rows = pl.ds(pl.multiple_of(c0 * RWKV_HEADS, RWKV_HEADS), RWKV_HEADS)
        return jnp.concatenate([ref[b, rows, :] for b in range(SCAN_BATCH)], axis=0)

    for src, dst in ((r_ref, r_s), (w_ref, w_s), (k_ref, k_s), (a_ref, a_s), (b_ref, b_s)):
        def key_tile(kq, carry, src=src, dst=dst):
            tile = jnp.concatenate([head_rows(src, kq), head_rows(src, HALF_HEAD + kq)], axis=0)
            dst[kq, :ts, :] = tile.T
            return carry

        lax.fori_loop(0, HALF_HEAD, key_tile, 0, unroll=RELAYOUT_UNROLL)

    value_rows = lambda v: pl.ds(v, ts, stride=RWKV_HEAD)

    def value_tile(v, carry):
        half = head_rows(v_ref, v)
        v_s[value_rows(v), :] = jnp.concatenate([half, half], axis=0).T
        return carry

    lax.fori_loop(0, RWKV_HEAD, value_tile, 0, unroll=RELAYOUT_UNROLL)

    scr = dict(r=r_s, w=w_s, k=k_s, a=a_s, b=b_s)
    step_rows = lambda t, rows: pl.ds(pl.multiple_of(t * RWKV_HEAD, RWKV_HEAD) + rows.start, rows.stop - rows.start)

    def put_y(t, rows, y):
        y_s[step_rows(t, rows), :] = y

    _scan_steps(st_ref, ts, lambda name, t, kq: scr[name][kq, pl.ds(t, 1), :],
                lambda t, rows: v_s[step_rows(t, rows), :], put_y)

    def y_tile(v, carry):
        tile = y_s[value_rows(v), :].T
        rows = pl.ds(pl.multiple_of(v * RWKV_HEADS, RWKV_HEADS), RWKV_HEADS)
        for b in range(SCAN_BATCH):
            y_ref[b, rows, :] = tile[b * RWKV_HEADS:(b + 1) * RWKV_HEADS, :]
        return carry

    lax.fori_loop(0, RWKV_HEAD, y_tile, 0, unroll=RELAYOUT_UNROLL)

    @pl.when(j == pl.num_programs(1) - 1)
    def _():
        sfin_ref[...] = st_ref[...]


def _wkv_scan_cm(r, w, k, a, b, v, s0):
    batch, _, seq = r.shape
    ts = LANES
    assert seq % ts == 0 and batch % SCAN_BATCH == 0
    cm_spec = pl.BlockSpec((SCAN_BATCH, RWKV_DIM, ts), lambda g, j: (g, 0, j))
    state_spec = pl.BlockSpec((None, HALF_HEAD, RWKV_HEAD, LANES), lambda g, j: (g, 0, 0, 0))
    return pl.pallas_call(
        functools.partial(_scan_kernel_cm, ts=ts),
        out_shape=[jax.ShapeDtypeStruct(r.shape, F32), jax.ShapeDtypeStruct(s0.shape, F32)],
        grid=(batch // SCAN_BATCH, seq // ts),
        in_specs=[cm_spec] * 6 + [state_spec],
        out_specs=[cm_spec, state_spec],
        scratch_shapes=[pltpu.VMEM((HALF_HEAD, RWKV_HEAD, LANES), F32)]
                       + [pltpu.VMEM((HALF_HEAD, ts + SCRATCH_ROW_PAD, LANES), F32)] * 5
                       + [pltpu.VMEM((RWKV_HEAD * ts, LANES), F32)] * 2,
        compiler_params=_cparams("parallel", "arbitrary"),
        name="rwkv7_state_scan",
    )(r, w, k, a, b, v, s0)


def _wkv_scan(r, w, k, a, b, v, s0):
    ng, seq = r.shape[:2]
    ts = _row_tile(seq, 128) if seq % 8 == 0 else seq
    half_spec = pl.BlockSpec((None, ts, HALF_HEAD, LANES), lambda g, j: (g, j, 0, 0))
    full_spec = pl.BlockSpec((None, ts, RWKV_HEAD, LANES), lambda g, j: (g, j, 0, 0))
    state_spec = pl.BlockSpec((None, HALF_HEAD, RWKV_HEAD, LANES), lambda g, j: (g, 0, 0, 0))
    return pl.pallas_call(
        functools.partial(_scan_kernel, ts=ts),
        out_shape=[jax.ShapeDtypeStruct(v.shape, F32), jax.ShapeDtypeStruct(s0.shape, F32)],
        grid=(ng, seq // ts),
        in_specs=[half_spec] * 5 + [full_spec, state_spec],
        out_specs=[full_spec, state_spec],
        scratch_shapes=[pltpu.VMEM((HALF_HEAD, RWKV_HEAD, LANES), F32)],
        compiler_params=_cparams("parallel", "arbitrary"),
        name="rwkv7_state_scan",
    )(r, w, k, a, b, v, s0)


def _to_scan_layout(x, batch, seq, dup):
    ng = batch // SCAN_BATCH
    if dup:
        x = x.reshape(ng, SCAN_BATCH, seq, RWKV_HEAD, RWKV_HEADS).transpose(0, 2, 3, 1, 4)
        x = x.reshape(ng, seq, RWKV_HEAD, SCAN_BATCH * RWKV_HEADS)
        return jnp.concatenate([x, x], axis=-1)
    x = x.reshape(ng, SCAN_BATCH, seq, 2, HALF_HEAD, RWKV_HEADS).transpose(0, 2, 4, 3, 1, 5)
    return x.reshape(ng, seq, HALF_HEAD, LANES)


def _from_scan_layout(y, batch, seq):
    ng = batch // SCAN_BATCH
    y = y[..., :LANES // 2].reshape(ng, seq, RWKV_HEAD, SCAN_BATCH, RWKV_HEADS).transpose(0, 3, 1, 2, 4)
    return y.reshape(batch * seq, RWKV_DIM)


def _state_to_scan_layout(s, batch):
    ng = batch // SCAN_BATCH
    s = s.reshape(ng, SCAN_BATCH, RWKV_HEADS, RWKV_HEAD, 2, HALF_HEAD).transpose(0, 5, 3, 4, 1, 2)
    return s.reshape(ng, HALF_HEAD, RWKV_HEAD, LANES)


def _state_from_scan_layout(s, batch):
    ng = batch // SCAN_BATCH
    s = s.reshape(ng, HALF_HEAD, RWKV_HEAD, 2, SCAN_BATCH, RWKV_HEADS).transpose(0, 4, 5, 2, 3, 1)
    return s.reshape(batch, RWKV_HEADS, RWKV_HEAD, RWKV_HEAD)


def _merge_kernel(h_ref, y_ref, bonus_ref, g_ref, oa_ref, gates_ref, lnw_ref, lnb_ref, ones_ref, wob_ref, woa_ref,
                  wout_ref, post_ref, o_ref, *, y_channel_major):
    ones = ones_ref[...]
    y = y_ref[...].T if y_channel_major else y_ref[...]
    inv_n = 1.0 / RWKV_HEAD
    mu = _segsum(y, ones) * inv_n
    yc = y - mu
    var = _segsum(yc * yc, ones) * inv_n
    yn = yc * lax.rsqrt(var + GN_EPS) * lnw_ref[...] + lnb_ref[...] + bonus_ref[...]
    o_b = _dot((yn * g_ref[...]).astype(BF16), wob_ref[...])
    o_a = _dot(oa_ref[...].astype(BF16), woa_ref[...])
    d = o_ref.shape[1]
    gates = gates_ref[...]
    mix = _sigmoid(gates[:, :d]) * o_a + _sigmoid(gates[:, d:]) * o_b
    o_ref[...] = h_ref[...] + _rms(_dot(mix.astype(BF16), wout_ref[...]), post_ref[...])


def _merge(h, y, bonus, g, oa, gates, lnw, lnb, ones, wob, woa, wout, post_g):
    t, d = h.shape
    tm = _row_tile(t, 512)
    consts = (lnw, lnb, ones, wob, woa, wout, post_g)
    y_spec = _rows_spec(tm, RWKV_DIM)
    if y.ndim == 3:
        tiles_per_seq = y.shape[2] // tm
        y_spec = pl.BlockSpec((None, RWKV_DIM, tm), lambda i: (i // tiles_per_seq, 0, i % tiles_per_seq))
    return pl.pallas_call(
        functools.partial(_merge_kernel, y_channel_major=y.ndim == 3),
        out_shape=jax.ShapeDtypeStruct((t, d), F32),
        grid=(t // tm,),
        in_specs=[_rows_spec(tm, d), y_spec] + [_rows_spec(tm, RWKV_DIM)] * 3 + [_rows_spec(tm, 2 * d)]
                 + [_const_spec(c.shape) for c in consts],
        out_specs=_rows_spec(tm, d),
        compiler_params=_cparams("parallel"),
        name="gated_merge_out_proj",
    )(h, y, bonus, g, oa, gates, *consts)


def _ple_kernel(h_ref, p_ref, pre_ref, wgate_ref, wproj_ref, post_ref, o_ref):
    h = h_ref[...]
    gp = _sigmoid(_dot(_rms(h, pre_ref[...]).astype(BF16), wgate_ref[...]))
    emb = _dot(p_ref[...].astype(BF16), wproj_ref[...])
    o_ref[...] = h + _rms(gp * emb, post_ref[...])


def _ple(h, p, pre_g, wgate, wproj, post_g):
    t, d = h.shape
    tm = _row_tile(t, 512)
    consts = (pre_g, wgate, wproj, post_g)
    return pl.pallas_call(
        _ple_kernel,
        out_shape=jax.ShapeDtypeStruct((t, d), F32),
        grid=(t // tm,),
        in_specs=[_rows_spec(tm, d), _rows_spec(tm, p.shape[1])] + [_const_spec(c.shape) for c in consts],
        out_specs=_rows_spec(tm, d),
        compiler_params=_cparams("parallel"),
        name="per_layer_embedding",
    )(h, p, *consts)


def _rope_tables(pos):
    inv = ROPE_THETA ** (-jnp.arange(0, ROPE_DIM, 2, dtype=F32) / ROPE_DIM)
    ang = pos[:, None] * inv[None, :]
    cos, sin = jnp.cos(ang), jnp.sin(ang)
    reps = ROPE_PAD // ROPE_DIM
    return jnp.tile(jnp.concatenate([cos, cos], axis=-1), (1, reps)), jnp.tile(jnp.concatenate([-sin, sin], axis=-1), (1, reps))


def _row(v):
    return v.reshape(1, -1).astype(F32)


def kernel(x_prompt, x_sample, p_prompt, p_sample, cache_ckv, cache_kpe, state_wkv, state_shift, page_table, ffn1_pre_g, ffn1_wg, ffn1_wu, ffn1_wd, ffn1_post_g, mix_pre_g, w_in, q_norm_g, w_uq, kv_norm_g, w_uk, w_uv, w_oa, rwkv_mu, rwkv_w0, rwkv_w_up, rwkv_a0, rwkv_a_up, rwkv_g_up, rwkv_k_k, rwkv_k_a, rwkv_r_k, rwkv_ln_w, rwkv_ln_b, w_ob, w_out, mix_post_g, ffn2_pre_g, ffn2_wg, ffn2_wu, ffn2_wd, ffn2_post_g, ple_pre_g, w_ple_gate, w_ple_proj, ple_post_g):
    depth = w_in.shape[0]
    assert depth == 1, "single-layer step"
    batch, seq, d_model = x_prompt.shape
    dec_batch, dec_seq, _ = x_sample.shape
    n_pages = page_table.shape[1]
    page = cache_ckv.shape[2]
    kv_rank = cache_ckv.shape[3]
    q_rank = w_uq.shape[1]
    shift_dim = state_shift.shape[2]
    tp, ts_tok = batch * seq, dec_batch * dec_seq
    assert batch % SCAN_BATCH == 0 and dec_batch % SCAN_BATCH == 0
    assert shift_dim == 3 * RWKV_DIM + W_LORA + A_LORA + G_LORA
    bf = lambda w: w.astype(BF16)

    w_in0 = w_in[0]
    c0, c1, c2, c3 = q_rank, q_rank + kv_rank, q_rank + kv_rank + ROPE_DIM, q_rank + kv_rank + ROPE_DIM + shift_dim
    wq, wkv, wrw, wgate = bf(w_in0[:, :c0]), bf(w_in0[:, c0:c1]), bf(w_in0[:, c2:c3]), bf(w_in0[:, c3:])
    wkpe = bf(jnp.pad(w_in0[:, c1:c2], ((0, 0), (0, ROPE_PAD - ROPE_DIM))))
    wuq3 = w_uq[0].reshape(q_rank, MLA_HEADS, NOPE_DIM + ROPE_DIM)
    wuq_nope = wuq3[:, :, :NOPE_DIM].reshape(q_rank, MLA_HEADS * NOPE_DIM)
    wuq_pe = jnp.pad(wuq3[:, :, NOPE_DIM:], ((0, 0), (0, 0), (0, ROPE_PAD - ROPE_DIM))).reshape(q_rank, MLA_HEADS * ROPE_PAD)
    wuq = bf(jnp.concatenate([wuq_nope, wuq_pe], axis=1))
    wukt = bf(w_uk[0].transpose(1, 2, 0))
    wuv = bf(w_uv[0].transpose(1, 0, 2))
    wuvt = bf(w_uv[0].transpose(1, 2, 0))
    def chan(x, inverse=False):
        a, b = (RWKV_HEAD, RWKV_HEADS) if inverse else (RWKV_HEADS, RWKV_HEAD)
        return jnp.swapaxes(x.reshape(x.shape[:-1] + (a, b)), -1, -2).reshape(x.shape)

    def chan_rw(x, inverse=False):
        rkv = x[..., :3 * RWKV_DIM].reshape(x.shape[:-1] + (3, RWKV_DIM))
        return jnp.concatenate([chan(rkv, inverse).reshape(x.shape[:-1] + (3 * RWKV_DIM,)), x[..., 3 * RWKV_DIM:]], axis=-1)

    wrw = chan_rw(wrw)
    wa_up = jnp.zeros((W_LORA + A_LORA, 2 * RWKV_DIM), F32)
    wa_up = bf(wa_up.at[:W_LORA, :RWKV_DIM].set(chan(rwkv_w_up[0])).at[W_LORA:, RWKV_DIM:].set(chan(rwkv_a_up[0])))
    seg = jnp.arange(RWKV_DIM) % RWKV_HEADS
    ones = (seg[:, None] == seg[None, :]).astype(BF16)
    ffn1 = (_row(ffn1_pre_g), bf(ffn1_wg[0]), bf(ffn1_wu[0]), bf(ffn1_wd[0]), _row(ffn1_post_g))
    ffn2 = (_row(ffn2_pre_g), bf(ffn2_wg[0]), bf(ffn2_wu[0]), bf(ffn2_wd[0]), _row(ffn2_post_g))
    inproj_w = (_row(mix_pre_g), wq, wkv, wkpe, wrw, wgate, _row(q_norm_g), wuq, _row(kv_norm_g), wukt)
    prep_w = (chan_rw(_row(rwkv_mu)), chan(_row(rwkv_w0)), chan(_row(rwkv_a0)), wa_up, bf(chan(rwkv_g_up[0])),
              chan(_row(rwkv_k_k)), chan(_row(rwkv_k_a)), chan(_row(rwkv_r_k)), ones)
    w_ob_rows = jnp.swapaxes(w_ob[0].reshape(RWKV_HEADS, RWKV_HEAD, -1), 0, 1).reshape(RWKV_DIM, -1)
    merge_w = (chan(_row(rwkv_ln_w)), chan(_row(rwkv_ln_b)), ones, bf(w_ob_rows), bf(w_oa[0]), bf(w_out[0]),
               _row(mix_post_g))
    last_row = lambda rw, nb, ns: chan_rw(rw.reshape(nb, ns, shift_dim)[:, -1], inverse=True)
    ple_w = (_row(ple_pre_g), bf(w_ple_gate[0]), bf(w_ple_proj[0]), _row(ple_post_g))

    def rwkv_mix(rw, shift0, s0, nb, ns):
        r, w, k, v, a, b, g, bonus = _rwkv_prep(rw, shift0, ns, *prep_w)
        if r.ndim == 3:
            y, s_fin = _wkv_scan_cm(r, w, k, a, b, v, _state_to_scan_layout(s0, nb))
            return y, bonus, g, _state_from_scan_layout(s_fin, nb)
        halves = [_to_scan_layout(t_, nb, ns, False) for t_ in (r, w, k, a, b)]
        y, s_fin = _wkv_scan(*halves, _to_scan_layout(v, nb, ns, True), _state_to_scan_layout(s0, nb))
        return _from_scan_layout(y, nb, ns), bonus, g, _state_from_scan_layout(s_fin, nb)

    tq = _row_tile(seq, 256)
    cos_p, sin_p = _rope_tables(jnp.arange(seq, dtype=F32))
    h1 = _ffn(x_prompt.reshape(tp, d_model), *ffn1)
    qcat, ckv_p, kpe_p, rw_p, gates, kcat, ckvt = _inproj(h1, *inproj_w, cos_p, sin_p, tq)
    oa = _attn_prompt(qcat, kcat, ckvt, wuvt, batch, seq, tq)
    y, bonus, g, wkv_p = rwkv_mix(rw_p, jnp.zeros((batch, shift_dim), F32),
                                  jnp.zeros((batch, RWKV_HEADS, RWKV_HEAD, RWKV_HEAD), F32), batch, seq)
    h2 = _merge(h1, y, bonus, g, oa, gates, *merge_w)
    h3 = _ffn(h2, *ffn2)
    y_prompt = _ple(h3, p_prompt[0].reshape(tp, -1), *ple_w)

    pos_s = jnp.tile(n_pages * page + jnp.arange(dec_seq, dtype=F32), dec_batch)
    cos_s, sin_s = _rope_tables(pos_s)
    h1 = _ffn(x_sample.reshape(ts_tok, d_model), *ffn1)
    qcat, ckv_s, kpe_s, rw_s, gates = _inproj(h1, *inproj_w, cos_s, sin_s, 0)
    qw = kv_rank + ROPE_PAD
    q3 = qcat.reshape(dec_batch, dec_seq, MLA_HEADS, qw).transpose(0, 2, 1, 3).reshape(
        dec_batch, MLA_HEADS * dec_seq, qw).astype(F32)
    o3 = _attn_paged(page_table, q3, ckv_s.reshape(dec_batch, dec_seq, kv_rank),
                     kpe_s.reshape(dec_batch, dec_seq, ROPE_DIM), cache_ckv, jnp.swapaxes(cache_kpe, 2, 3))
    o_lat = o3.reshape(dec_batch, MLA_HEADS, dec_seq, kv_rank).transpose(0, 2, 1, 3).reshape(ts_tok, MLA_HEADS * kv_rank)
    oa = _uv_proj(o_lat, wuv)
    y, bonus, g, wkv_s = rwkv_mix(rw_s, chan_rw(state_shift[0]), state_wkv[0].astype(F32), dec_batch, dec_seq)
    h2 = _merge(h1, y, bonus, g, oa, gates, *merge_w)
    h3 = _ffn(h2, *ffn2)
    y_sample = _ple(h3, p_sample[0].reshape(ts_tok, -1), *ple_w)

    return (y_prompt.reshape(batch, seq, d_model), y_sample.reshape(dec_batch, dec_seq, d_model),
            ckv_p.reshape(1, batch, seq, kv_rank), kpe_p.reshape(1, batch, seq, ROPE_DIM),
            wkv_p[None], last_row(rw_p, batch, seq)[None],
            ckv_s.reshape(1, dec_batch, dec_seq, kv_rank), kpe_s.reshape(1, dec_batch, dec_seq, ROPE_DIM),
            wkv_s[None], last_row(rw_s, dec_batch, dec_seq)[None])
```
